```python
import jax, jax.numpy as jnp
from jax import lax
import numpy as np

D_MODEL = 2048
BATCH = 1
SEQ = 8192
DEPTH = 4

EPS = 1e-6
D_A = D_MODEL // 2
CONV_WIDTH = 31
D_B = D_MODEL // 2
CHUNK = 128
HEAD_DIM_B = 128
N_HEADS_B = D_B // HEAD_DIM_B
D_IN_AB = 2 * D_A + 2 * D_B
POOL_WINDOWS = (2, 4, 8, 16)
N_POOL_GROUPS = len(POOL_WINDOWS)
D_POOL_GROUP = D_MODEL // N_POOL_GROUPS
D_FF = -(-8 * D_MODEL // (3 * 256)) * 256
N_MOD = 6
N_EVEN = (DEPTH + 1) // 2
N_ODD = DEPTH // 2

kernel_name = "hybrid_conv_gmlp_pool_adaln_trunk"


def rmsnorm(x, g):
    xf = x.astype(jnp.float32)
    y = xf * lax.rsqrt(jnp.mean(xf * xf, axis=-1, keepdims=True) + EPS)
    return (y * g.astype(jnp.float32)).astype(x.dtype)


def layernorm(x, g, b):
    xf = x.astype(jnp.float32)
    mu = jnp.mean(xf, axis=-1, keepdims=True)
    xc = xf - mu
    y = xc * lax.rsqrt(jnp.mean(xc * xc, axis=-1, keepdims=True) + EPS)
    return (y * g.astype(jnp.float32) + b.astype(jnp.float32)).astype(x.dtype)


def modulate(h, shift, scale):
    return h * (1.0 + scale[:, None, :]) + shift[:, None, :]


def mixer_ab(h, w_in, conv_w, conv_b, a_g, a_b, v_g, v_b, w_s, s_bias, w_out):
    bsz, s, _ = h.shape
    proj = h @ w_in
    a_val = proj[..., :D_A]
    a_gate = proj[..., D_A:2 * D_A]
    b_u = proj[..., 2 * D_A:2 * D_A + D_B]
    b_v = proj[..., 2 * D_A + D_B:]

    a = a_val * jax.nn.sigmoid(a_gate)
    a = lax.conv_general_dilated(
        a, conv_w[:, None, :].astype(a.dtype), window_strides=(1,),
        padding=[(CONV_WIDTH - 1, 0)],
        dimension_numbers=('NWC', 'WIO', 'NWC'),
        feature_group_count=D_A) + conv_b
    a = jax.nn.silu(layernorm(a, a_g, a_b))

    v = layernorm(b_v, v_g, v_b).reshape(bsz, s // CHUNK, CHUNK, N_HEADS_B, HEAD_DIM_B)
    causal = jnp.tril(jnp.ones((CHUNK, CHUNK), dtype=bool))
    w_c = jnp.where(causal[None], w_s, jnp.zeros_like(w_s))
    v = jnp.einsum('hts,bnshd->bnthd', w_c, v) + s_bias.T[:, :, None]
    b_out = b_u * v.reshape(bsz, s, D_B)

    return jnp.concatenate([a, b_out], axis=-1) @ w_out


def mixer_c(h, pool_w, pool_scale):
    bsz, s, _ = h.shape
    hf = h.astype(jnp.float32)
    cs = jnp.cumsum(hf, axis=1)
    count_base = (jnp.arange(s) + 1).astype(jnp.float32)
    pooled = []
    for gi, w in enumerate(POOL_WINDOWS):
        sl = slice(gi * D_POOL_GROUP, (gi + 1) * D_POOL_GROUP)
        cg = cs[..., sl]
        lagged = jnp.pad(cg[:, :s - w], ((0, 0), (w, 0), (0, 0)))
        cnt = jnp.minimum(count_base, float(w))[None, :, None]
        pooled.append(((cg - lagged) / cnt - hf[..., sl]).astype(h.dtype))
    p = jnp.stack(pooled, axis=2)
    y = jnp.einsum('bsgi,gio->bsgo', p, pool_w).reshape(bsz, s, D_MODEL)
    return y * pool_scale


def swiglu(h, w1, w3, w2):
    return (jax.nn.silu(h @ w1) * (h @ w3)) @ w2


def setup_inputs(seed: int = 0) -> dict:
    key = jax.random.key(seed)
    ks = jax.random.split(key, 24)
    f32 = jnp.float32
    nrm = lambda k, shape, s: (jax.random.normal(k, shape, f32) * s)
    gain = lambda k, shape: 1.0 + 0.02 * jax.random.normal(k, shape, f32)
    return {
        "x": nrm(ks[0], (BATCH, SEQ, D_MODEL), 1.0),
        "c": nrm(ks[1], (BATCH, D_MODEL), 1.0),
        "ada_w": nrm(ks[2], (DEPTH, D_MODEL, N_MOD * D_MODEL), 0.5 * D_MODEL ** -0.5),
        "ada_b": nrm(ks[3], (DEPTH, N_MOD * D_MODEL), 0.02),
        "norm_mix_g": gain(ks[4], (DEPTH, D_MODEL)),
        "norm_ffn_g": gain(ks[5], (DEPTH, D_MODEL)),
        "ab_w_in": nrm(ks[6], (N_EVEN, D_MODEL, D_IN_AB), D_MODEL ** -0.5),
        "a_conv_w": nrm(ks[7], (N_EVEN, CONV_WIDTH, D_A), CONV_WIDTH ** -0.5),
        "a_conv_b": nrm(ks[8], (N_EVEN, D_A), 0.02),
        "a_norm_g": gain(ks[9], (N_EVEN, D_A)),
        "a_norm_b": nrm(ks[10], (N_EVEN, D_A), 0.02),
        "b_norm_g": gain(ks[11], (N_EVEN, D_B)),
        "b_norm_b": nrm(ks[12], (N_EVEN, D_B), 0.02),
        "b_w_s": nrm(ks[13], (N_EVEN, N_HEADS_B, CHUNK, CHUNK), CHUNK ** -0.5),
        "b_bias": gain(ks[14], (N_EVEN, N_HEADS_B, CHUNK)),
        "ab_w_out": nrm(ks[15], (N_EVEN, D_A + D_B, D_MODEL), (D_A + D_B) ** -0.5),
        "pool_w": nrm(ks[16], (N_ODD, N_POOL_GROUPS, D_POOL_GROUP, D_POOL_GROUP), D_POOL_GROUP ** -0.5),
        "pool_scale": 1.0 + 0.1 * jax.random.normal(ks[17], (N_ODD, D_MODEL), f32),
        "ffn_w1": nrm(ks[18], (DEPTH, D_MODEL, D_FF), D_MODEL ** -0.5),
        "ffn_w3": nrm(ks[19], (DEPTH, D_MODEL, D_FF), D_MODEL ** -0.5),
        "ffn_w2": nrm(ks[20], (DEPTH, D_FF, D_MODEL), D_FF ** -0.5),
        "final_g": gain(ks[21], (D_MODEL,)),
    }


def reference(x, c, ada_w, ada_b, norm_mix_g, norm_ffn_g, ab_w_in, a_conv_w, a_conv_b,
              a_norm_g, a_norm_b, b_norm_g, b_norm_b, b_w_s, b_bias, ab_w_out,
              pool_w, pool_scale, ffn_w1, ffn_w3, ffn_w2, final_g):
    cond = jax.nn.silu(c)
    for l in range(DEPTH):
        mod = cond @ ada_w[l] + ada_b[l]
        sh1, sc1, g1, sh2, sc2, g2 = jnp.split(mod, N_MOD, axis=-1)
        h = modulate(rmsnorm(x, norm_mix_g[l]), sh1, sc1)
        i = l // 2
        if l % 2 == 0:
            y = mixer_ab(h, ab_w_in[i], a_conv_w[i], a_conv_b[i], a_norm_g[i], a_norm_b[i],
                         b_norm_g[i], b_norm_b[i], b_w_s[i], b_bias[i], ab_w_out[i])
        else:
            y = mixer_c(h, pool_w[i], pool_scale[i])
        x = x + g1[:, None, :] * y
        h = modulate(rmsnorm(x, norm_ffn_g[l]), sh2, sc2)
        x = x + g2[:, None, :] * swiglu(h, ffn_w1[l], ffn_w3[l], ffn_w2[l])
    return rmsnorm(x, final_g)
```

```python
import functools

import jax
import jax.numpy as jnp
from jax import lax
from jax.experimental import pallas as pl
from jax.experimental.pallas import tpu as pltpu

F32 = jnp.float32
BF16 = jnp.bfloat16

EPS = 1e-6
CONV_WIDTH = 31
CHUNK = 128
HEAD_DIM_B = 128
POOL_WINDOWS = (2, 4, 8, 16)
N_MOD = 6

HALO = 32
VMEM_LIMIT_V7X = 56 * 1024 * 1024


def _params(semantics):
    return pltpu.CompilerParams(dimension_semantics=semantics, vmem_limit_bytes=VMEM_LIMIT_V7X)


def _resident(shape, index_map):
    return pl.BlockSpec(shape, index_map, pipeline_mode=pl.Buffered(1))


def _bdot(a, b):
    return jnp.dot(a, b, preferred_element_type=F32)


def _modulated_rmsnorm(x, gain, shift, scale):
    ms = jnp.mean(x * x, axis=-1, keepdims=True)
    return x * lax.rsqrt(ms + EPS) * (gain * (1.0 + scale)) + shift


def _layernorm(x, g, b):
    mu = jnp.mean(x, axis=-1, keepdims=True)
    xc = x - mu
    var = jnp.mean(xc * xc, axis=-1, keepdims=True)
    return xc * lax.rsqrt(var + EPS) * g + b


def _adaln_kernel(c_ref, w_ref, b_ref, o_ref):
    c = c_ref[...]
    cond = c * jax.nn.sigmoid(c)
    cond8 = jnp.broadcast_to(cond, (8, cond.shape[1])).astype(BF16)
    y = _bdot(cond8, w_ref[...].astype(BF16))
    o_ref[...] = y[0:1, :] + b_ref[...]


def _adaln(c, ada_w, ada_b, tn=1024):
    depth, d, n = ada_w.shape
    out = pl.pallas_call(
        _adaln_kernel,
        grid=(depth, n // tn),
        in_specs=[
            pl.BlockSpec((1, d), lambda l, j: (0, 0)),
            pl.BlockSpec((None, d, tn), lambda l, j: (l, 0, j)),
            pl.BlockSpec((None, 1, tn), lambda l, j: (l, 0, j)),
        ],
        out_specs=pl.BlockSpec((None, 1, tn), lambda l, j: (l, 0, j)),
        out_shape=jax.ShapeDtypeStruct((depth, 1, n), F32),
        compiler_params=_params(("arbitrary", "arbitrary")),
        name="adaln",
    )(c, ada_w, ada_b.reshape(depth, 1, n))
    return out.reshape(depth, N_MOD, d)


def _even_kernel(x_ref, mod_ref, g_ref, win_ref, cw_ref, cb_ref, ag_ref, ab_ref, vg_ref, vb_ref,
                 ws_ref, bias_ref, wout_ref, o_ref, abuf_ref):
    tm = x_ref.shape[0]
    d_a = cw_ref.shape[1]
    n_heads = ws_ref.shape[0]
    n_chunks = tm // CHUNK

    @pl.when(pl.program_id(0) == 0)
    def _():
        abuf_ref[0:HALO, :] = jnp.zeros((HALO, d_a), F32)

    x = x_ref[...]
    m = mod_ref[...]
    shift, scale, gate = m[0:1], m[1:2], m[2:3]
    h = _modulated_rmsnorm(x, g_ref[...], shift, scale)
    proj = _bdot(h.astype(BF16), win_ref[...])

    a = proj[:, 0:d_a] * jax.nn.sigmoid(proj[:, d_a:2 * d_a])
    abuf_ref[HALO:HALO + tm, :] = a
    first = HALO - (CONV_WIDTH - 1)
    acc = jnp.zeros((tm, d_a), F32) + cb_ref[...]
    for k in range(CONV_WIDTH):
        acc = acc + cw_ref[k:k + 1, :] * abuf_ref[first + k:first + k + tm, :]
    abuf_ref[0:HALO, :] = abuf_ref[tm:tm + HALO, :]
    a_ln = _layernorm(acc, ag_ref[...], ab_ref[...])
    a_out = a_ln * jax.nn.sigmoid(a_ln)

    b_u = proj[:, 2 * d_a:2 * d_a + n_heads * HEAD_DIM_B]
    b_v = proj[:, 2 * d_a + n_heads * HEAD_DIM_B:]
    v = _layernorm(b_v, vg_ref[...], vb_ref[...]).astype(BF16)
    row = lax.broadcasted_iota(jnp.int32, (CHUNK, CHUNK), 0)
    col = lax.broadcasted_iota(jnp.int32, (CHUNK, CHUNK), 1)
    causal = row >= col
    mixed = []
    for hd in range(n_heads):
        lanes = slice(hd * HEAD_DIM_B, (hd + 1) * HEAD_DIM_B)
        rhs = jnp.concatenate([v[c * CHUNK:(c + 1) * CHUNK, lanes] for c in range(n_chunks)], axis=1)
        w_c = jnp.where(causal, ws_ref[hd], 0.0).astype(BF16)
        mixed.append(_bdot(w_c, rhs))
    bias = bias_ref[...]
    rows = []
    for c in range(n_chunks):
        lanes = slice(c * HEAD_DIM_B, (c + 1) * HEAD_DIM_B)
        rows.append(jnp.concatenate([mixed[hd][:, lanes] for hd in range(n_heads)], axis=1) + bias)
    b_out = b_u * jnp.concatenate(rows, axis=0)

    cat = jnp.concatenate([a_out, b_out], axis=1).astype(BF16)
    y = _bdot(cat, wout_ref[...])
    o_ref[...] = x + gate * y


def _even_layer(x2, mod, norm_g, w_in, conv_w, conv_b, a_g, a_b, v_g, v_b, w_s, bias_full, w_out, l, tm=256):
    s, d = x2.shape
    i = l // 2
    d_in = w_in.shape[2]
    d_a = conv_w.shape[2]
    d_b = v_g.shape[2]
    n_heads = w_s.shape[1]
    vec = lambda width, idx: pl.BlockSpec((None, 1, width), lambda t: (idx, 0, 0))
    return pl.pallas_call(
        _even_kernel,
        grid=(s // tm,),
        in_specs=[
            pl.BlockSpec((tm, d), lambda t: (t, 0)),
            pl.BlockSpec((None, N_MOD, d), lambda t: (l, 0, 0)),
            vec(d, l),
            _resident((None, d, d_in), lambda t: (i, 0, 0)),
            pl.BlockSpec((None, CONV_WIDTH, d_a), lambda t: (i, 0, 0)),
            vec(d_a, i), vec(d_a, i), vec(d_a, i), vec(d_b, i), vec(d_b, i),
            pl.BlockSpec((None, n_heads, CHUNK, CHUNK), lambda t: (i, 0, 0, 0)),
            pl.BlockSpec((None, CHUNK, d_b), lambda t: (i, 0, 0)),
            _resident((None, d_a + d_b, d), lambda t: (i, 0, 0)),
        ],
        out_specs=pl.BlockSpec((tm, d), lambda t: (t, 0)),
        out_shape=jax.ShapeDtypeStruct((s, d), F32),
        scratch_shapes=[pltpu.VMEM((HALO + tm, d_a), F32)],
        compiler_params=_params(("arbitrary",)),
        name="even_mixer",
    )(x2, mod, norm_g, w_in, conv_w, conv_b, a_g, a_b, v_g, v_b, w_s, bias_full, w_out)


def _pool_kernel(x_ref, mod_ref, g_ref, pw_ref, ps_ref, o_ref, hbuf_ref, p_ref, q_ref):
    tm, d = x_ref.shape
    n_groups = len(POOL_WINDOWS)
    dg = d // n_groups
    t0 = pl.program_id(0) * tm

    @pl.when(pl.program_id(0) == 0)
    def _():
        hbuf_ref[0:HALO, :] = jnp.zeros((HALO, d), F32)

    x = x_ref[...]
    m = mod_ref[...]
    shift, scale, gate = m[0:1], m[1:2], m[2:3]
    h = _modulated_rmsnorm(x, g_ref[...], shift, scale)
    hbuf_ref[HALO:HALO + tm, :] = h

    end = HALO + tm
    p_ref[8:end, :] = hbuf_ref[8:end, :] + hbuf_ref[7:end - 1, :]
    q_ref[16:end, dg:] = p_ref[16:end, dg:] + p_ref[14:end - 2, dg:]
    p_ref[24:end, 2 * dg:] = q_ref[24:end, 2 * dg:] + q_ref[20:end - 4, 2 * dg:]
    q_ref[32:end, 3 * dg:] = p_ref[32:end, 3 * dg:] + p_ref[24:end - 8, 3 * dg:]
    sums = (p_ref, q_ref, p_ref, q_ref)

    pos = (lax.broadcasted_iota(jnp.int32, (tm, dg), 0) + (t0 + 1)).astype(F32)
    ys = []
    for gi, w in enumerate(POOL_WINDOWS):
        lanes = slice(gi * dg, (gi + 1) * dg)
        cnt = jnp.minimum(pos, float(w))
        pooled = sums[gi][HALO:end, lanes] / cnt - h[:, lanes]
        ys.append(_bdot(pooled.astype(BF16), pw_ref[gi]))
    y = jnp.concatenate(ys, axis=1) * ps_ref[...]
    o_ref[...] = x + gate * y
    hbuf_ref[0:HALO, :] = hbuf_ref[tm:tm + HALO, :]


def _pool_layer(x2, mod, norm_g, pool_w, pool_scale, l, tm=512):
    s, d = x2.shape
    i = l // 2
    n_groups, dg, _ = pool_w.shape[1:]
    return pl.pallas_call(
        _pool_kernel,
        grid=(s // tm,),
        in_specs=[
            pl.BlockSpec((tm, d), lambda t: (t, 0)),
            pl.BlockSpec((None, N_MOD, d), lambda t: (l, 0, 0)),
            pl.BlockSpec((None, 1, d), lambda t: (l, 0, 0)),
            _resident((None, n_groups, dg, dg), lambda t: (i, 0, 0, 0)),
            pl.BlockSpec((None, 1, d), lambda t: (i, 0, 0)),
        ],
        out_specs=pl.BlockSpec((tm, d), lambda t: (t, 0)),
        out_shape=jax.ShapeDtypeStruct((s, d), F32),
        scratch_shapes=[pltpu.VMEM((HALO + tm, d), F32)] * 3,
        compiler_params=_params(("arbitrary",)),
        name="pool_mixer",
    )(x2, mod, norm_g, pool_w, pool_scale)


def _ffn_kernel(x_ref, mod_ref, g_ref, w1_ref, w3_ref, w2_ref, fg_ref, o_ref, h_ref, *, final_norm):
    j = pl.program_id(1)
    last = pl.num_programs(1) - 1

    @pl.when(j == 0)
    def _():
        m = mod_ref[...]
        h = _modulated_rmsnorm(x_ref[...], g_ref[...], m[3:4], m[4:5])
        h_ref[...] = h.astype(BF16)

    h = h_ref[...]
    a = _bdot(h, w1_ref[...])
    b = _bdot(h, w3_ref[...])
    u = (a * jax.nn.sigmoid(a) * b).astype(BF16)
    y = _bdot(u, w2_ref[...])

    @pl.when(j == 0)
    def _():
        o_ref[...] = y

    @pl.when(j > 0)
    def _():
        o_ref[...] += y

    @pl.when(j == last)
    def _():
        out = x_ref[...] + mod_ref[5:6, :] * o_ref[...]
        if final_norm:
            ms = jnp.mean(out * out, axis=-1, keepdims=True)
            out = out * lax.rsqrt(ms + EPS) * fg_ref[...]
        o_ref[...] = out


def _ffn_layer(x2, mod, norm_g, w1, w3, w2, final_g, l, final_norm, tm=512, tf=512):
    s, d = x2.shape
    d_ff = w1.shape[2]
    return pl.pallas_call(
        functools.partial(_ffn_kernel, final_norm=final_norm),
        grid=(s // tm, d_ff // tf),
        in_specs=[
            pl.BlockSpec((tm, d), lambda t, j: (t, 0)),
            pl.BlockSpec((None, N_MOD, d), lambda t, j: (l, 0, 0)),
            pl.BlockSpec((None, 1, d), lambda t, j: (l, 0, 0)),
            pl.BlockSpec((None, d, tf), lambda t, j: (l, 0, j)),
            pl.BlockSpec((None, d, tf), lambda t, j: (l, 0, j)),
            pl.BlockSpec((None, tf, d), lambda t, j: (l, j, 0)),
            pl.BlockSpec((1, d), lambda t, j: (0, 0)),
        ],
        out_specs=pl.BlockSpec((tm, d), lambda t, j: (t, 0)),
        out_shape=jax.ShapeDtypeStruct((s, d), F32),
        scratch_shapes=[pltpu.VMEM((tm, d), BF16)],
        compiler_params=_params(("arbitrary", "arbitrary")),
        name="ffn",
    )(x2, mod, norm_g, w1, w3, w2, final_g)


def kernel(x, c, ada_w, ada_b, norm_mix_g, norm_ffn_g, ab_w_in, a_conv_w, a_conv_b, a_norm_g, a_norm_b,
           b_norm_g, b_norm_b, b_w_s, b_bias, ab_w_out, pool_w, pool_scale, ffn_w1, ffn_w3, ffn_w2, final_g):
    bsz, s, d = x.shape
    depth = ada_w.shape[0]
    assert bsz == 1, "per-sequence adaLN terms are passed as single rows"
    row3 = lambda p: p.reshape(p.shape[0], 1, p.shape[1])

    mod = _adaln(c, ada_w, ada_b)
    n_even, n_heads, chunk = b_bias.shape
    bias_full = jnp.broadcast_to(
        jnp.transpose(b_bias, (0, 2, 1))[:, :, :, None], (n_even, chunk, n_heads, HEAD_DIM_B)
    ).reshape(n_even, chunk, n_heads * HEAD_DIM_B)

    w_in, w_out, p_w = ab_w_in.astype(BF16), ab_w_out.astype(BF16), pool_w.astype(BF16)
    w1, w3, w2 = ffn_w1.astype(BF16), ffn_w3.astype(BF16), ffn_w2.astype(BF16)
    mix_g, ffn_g = row3(norm_mix_g), row3(norm_ffn_g)
    conv_b, a_g, a_b, v_g, v_b = (row3(p) for p in (a_conv_b, a_norm_g, a_norm_b, b_norm_g, b_norm_b))
    p_scale = row3(pool_scale)
    fg = final_g.reshape(1, d)

    x2 = x.reshape(s, d)
    for l in range(depth):
        if l % 2 == 0:
            x2 = _even_layer(x2, mod, mix_g, w_in, a_conv_w, conv_b, a_g, a_b, v_g, v_b, b_w_s, bias_full,
                             w_out, l)
        else:
            x2 = _pool_layer(x2, mod, mix_g, p_w, p_scale, l)
        x2 = _ffn_layer(x2, mod, ffn_g, w1, w3, w2, fg, l, final_norm=(l == depth - 1))
    return x2.reshape(bsz, s, d)
```

```python
import functools

import jax
import jax.numpy as jnp
from jax import lax
from jax.experimental import pallas as pl
from jax.experimental.pallas import tpu as pltpu

F32 = jnp.float32
BF16 = jnp.bfloat16

EPS = 1e-6
CONV_WIDTH = 31
CHUNK = 128
HEAD_DIM_B = 128
POOL_WINDOWS = (2, 4, 8, 16)
N_MOD = 6

HALO = 32
SUBLANES = 8
VMEM_LIMIT_V7X = 56 * 1024 * 1024


def _params(semantics):
    return pltpu.CompilerParams(dimension_semantics=semantics, vmem_limit_bytes=VMEM_LIMIT_V7X)


def _resident(shape, index_map):
    return pl.BlockSpec(shape, index_map, pipeline_mode=pl.Buffered(1))


def _bdot(a, b):
    return jnp.dot(a, b, preferred_element_type=F32)


def _modulated_rmsnorm(x, gain, shift, scale):
    ms = jnp.mean(x * x, axis=-1, keepdims=True)
    return x * lax.rsqrt(ms + EPS) * (gain * (1.0 + scale)) + shift


def _layernorm(x, g, b):
    mu = jnp.mean(x, axis=-1, keepdims=True)
    xc = x - mu
    var = jnp.mean(xc * xc, axis=-1, keepdims=True)
    return xc * lax.rsqrt(var + EPS) * g + b


def _adaln_kernel(c_ref, w_ref, b_ref, o_ref):
    c = c_ref[...]
    cond = c * jax.nn.sigmoid(c)
    cond8 = jnp.broadcast_to(cond, (8, cond.shape[1])).astype(BF16)
    y = _bdot(cond8, w_ref[...].astype(BF16))
    o_ref[...] = y[0:1, :] + b_ref[...]


def _adaln(c, ada_w, ada_b, tn=1024):
    depth, d, n = ada_w.shape
    out = pl.pallas_call(
        _adaln_kernel,
        grid=(depth, n // tn),
        in_specs=[
            pl.BlockSpec((1, d), lambda l, j: (0, 0)),
            pl.BlockSpec((None, d, tn), lambda l, j: (l, 0, j)),
            pl.BlockSpec((None, 1, tn), lambda l, j: (l, 0, j)),
        ],
        out_specs=pl.BlockSpec((None, 1, tn), lambda l, j: (l, 0, j)),
        out_shape=jax.ShapeDtypeStruct((depth, 1, n), F32),
        compiler_params=_params(("arbitrary", "arbitrary")),
        name="adaln",
    )(c, ada_w, ada_b.reshape(depth, 1, n))
    return out.reshape(depth, N_MOD, d)


def _even_kernel(x_ref, mod_ref, g_ref, win_ref, cw_ref, cb_ref, ag_ref, ab_ref, vg_ref, vb_ref,
                 ws_ref, bias_ref, wout_ref, o_ref, abuf_ref):
    tm = x_ref.shape[0]
    d_a = cw_ref.shape[1]
    n_heads = ws_ref.shape[0]
    n_chunks = tm // CHUNK

    @pl.when(pl.program_id(0) == 0)
    def _():
        abuf_ref[0:HALO, :] = jnp.zeros((HALO, d_a), F32)
        abuf_ref[HALO + tm:HALO + tm + SUBLANES, :] = jnp.zeros((SUBLANES, d_a), F32)

    x = x_ref[...]
    m = mod_ref[...]
    shift, scale, gate = m[0:1], m[1:2], m[2:3]
    h = _modulated_rmsnorm(x, g_ref[...], shift, scale)
    proj = _bdot(h.astype(BF16), win_ref[...])

    a = proj[:, 0:d_a] * jax.nn.sigmoid(proj[:, d_a:2 * d_a])
    abuf_ref[HALO:HALO + tm, :] = a
    first = HALO - (CONV_WIDTH - 1)
    acc = cb_ref[...]
    for r in range(SUBLANES):
        p_r = None
        for q in range((first + CONV_WIDTH - 1) // SUBLANES + 1):
            k = SUBLANES * q + r - first
            if 0 <= k < CONV_WIDTH:
                term = cw_ref[k:k + 1, :] * abuf_ref[SUBLANES * q:SUBLANES * q + tm + SUBLANES, :]
                p_r = term if p_r is None else p_r + term
        acc = acc + p_r[r:r + tm, :]
    abuf_ref[0:HALO, :] = abuf_ref[tm:tm + HALO, :]
    a_ln = _layernorm(acc, ag_ref[...], ab_ref[...])
    a_out = a_ln * jax.nn.sigmoid(a_ln)

    b_u = proj[:, 2 * d_a:2 * d_a + n_heads * HEAD_DIM_B]
    b_v = proj[:, 2 * d_a + n_heads * HEAD_DIM_B:]
    v = _layernorm(b_v, vg_ref[...], vb_ref[...]).astype(BF16)
    row = lax.broadcasted_iota(jnp.int32, (CHUNK, CHUNK), 0)
    col = lax.broadcasted_iota(jnp.int32, (CHUNK, CHUNK), 1)
    causal = row >= col
    mixed = []
    for hd in range(n_heads):
        lanes = slice(hd * HEAD_DIM_B, (hd + 1) * HEAD_DIM_B)
        rhs = jnp.concatenate([v[c * CHUNK:(c + 1) * CHUNK, lanes] for c in range(n_chunks)], axis=1)
        w_c = jnp.where(causal, ws_ref[hd], 0.0).astype(BF16)
        mixed.append(_bdot(w_c, rhs))
    bias = bias_ref[...]
    rows = []
    for c in range(n_chunks):
        lanes = slice(c * HEAD_DIM_B, (c + 1) * HEAD_DIM_B)
        rows.append(jnp.concatenate([mixed[hd][:, lanes] for hd in range(n_heads)], axis=1) + bias)
    b_out = b_u * jnp.concatenate(rows, axis=0)

    cat = jnp.concatenate([a_out, b_out], axis=1).astype(BF16)
    y = _bdot(cat, wout_ref[...])
    o_ref[...] = x + gate * y


def _even_layer(x2, mod, norm_g, w_in, conv_w, conv_b, a_g, a_b, v_g, v_b, w_s, bias_full, w_out, l, tm=256):
    s, d = x2.shape
    i = l // 2
    d_in = w_in.shape[2]
    d_a = conv_w.shape[2]
    d_b = v_g.shape[2]
    n_heads = w_s.shape[1]
    vec = lambda width, idx: pl.BlockSpec((None, 1, width), lambda t: (idx, 0, 0))
    return pl.pallas_call(
        _even_kernel,
        grid=(s // tm,),
        in_specs=[
            pl.BlockSpec((tm, d), lambda t: (t, 0)),
            pl.BlockSpec((None, N_MOD, d), lambda t: (l, 0, 0)),
            vec(d, l),
            _resident((None, d, d_in), lambda t: (i, 0, 0)),
            pl.BlockSpec((None, CONV_WIDTH, d_a), lambda t: (i, 0, 0)),
            vec(d_a, i), vec(d_a, i), vec(d_a, i), vec(d_b, i), vec(d_b, i),
            pl.BlockSpec((None, n_heads, CHUNK, CHUNK), lambda t: (i, 0, 0, 0)),
            pl.BlockSpec((None, CHUNK, d_b), lambda t: (i, 0, 0)),
            _resident((None, d_a + d_b, d), lambda t: (i, 0, 0)),
        ],
        out_specs=pl.BlockSpec((tm, d), lambda t: (t, 0)),
        out_shape=jax.ShapeDtypeStruct((s, d), F32),
        scratch_shapes=[pltpu.VMEM((HALO + tm + SUBLANES, d_a), F32)],
        compiler_params=_params(("arbitrary",)),
        name="even_mixer",
    )(x2, mod, norm_g, w_in, conv_w, conv_b, a_g, a_b, v_g, v_b, w_s, bias_full, w_out)


def _pool_kernel(x_ref, mod_ref, g_ref, pw_ref, ps_ref, o_ref, hbuf_ref, p_ref, q_ref):
    tm, d = x_ref.shape
    n_groups = len(POOL_WINDOWS)
    dg = d // n_groups
    t0 = pl.program_id(0) * tm

    @pl.when(pl.program_id(0) == 0)
    def _():
        hbuf_ref[0:HALO, :] = jnp.zeros((HALO, d), F32)

    x = x_ref[...]
    m = mod_ref[...]
    shift, scale, gate = m[0:1], m[1:2], m[2:3]
    h = _modulated_rmsnorm(x, g_ref[...], shift, scale)
    hbuf_ref[HALO:HALO + tm, :] = h

    end = HALO + tm
    p_ref[8:end, :] = hbuf_ref[8:end, :] + hbuf_ref[7:end - 1, :]
    q_ref[16:end, dg:] = p_ref[16:end, dg:] + p_ref[14:end - 2, dg:]
    p_ref[24:end, 2 * dg:] = q_ref[24:end, 2 * dg:] + q_ref[20:end - 4, 2 * dg:]
    q_ref[32:end, 3 * dg:] = p_ref[32:end, 3 * dg:] + p_ref[24:end - 8, 3 * dg:]
    sums = (p_ref, q_ref, p_ref, q_ref)

    pos = (lax.broadcasted_iota(jnp.int32, (tm, dg), 0) + (t0 + 1)).astype(F32)
    ys = []
    for gi, w in enumerate(POOL_WINDOWS):
        lanes = slice(gi * dg, (gi + 1) * dg)
        cnt = jnp.minimum(pos, float(w))
        pooled = sums[gi][HALO:end, lanes] / cnt - h[:, lanes]
        ys.append(_bdot(pooled.astype(BF16), pw_ref[gi]))
    y = jnp.concatenate(ys, axis=1) * ps_ref[...]
    o_ref[...] = x + gate * y
    hbuf_ref[0:HALO, :] = hbuf_ref[tm:tm + HALO, :]


def _pool_layer(x2, mod, norm_g, pool_w, pool_scale, l, tm=512):
    s, d = x2.shape
    i = l // 2
    n_groups, dg, _ = pool_w.shape[1:]
    return pl.pallas_call(
        _pool_kernel,
        grid=(s // tm,),
        in_specs=[
            pl.BlockSpec((tm, d), lambda t: (t, 0)),
            pl.BlockSpec((None, N_MOD, d), lambda t: (l, 0, 0)),
            pl.BlockSpec((None, 1, d), lambda t: (l, 0, 0)),
            _resident((None, n_groups, dg, dg), lambda t: (i, 0, 0, 0)),
            pl.BlockSpec((None, 1, d), lambda t: (i, 0, 0)),
        ],
        out_specs=pl.BlockSpec((tm, d), lambda t: (t, 0)),
        out_shape=jax.ShapeDtypeStruct((s, d), F32),
        scratch_shapes=[pltpu.VMEM((HALO + tm, d), F32)] * 3,
        compiler_params=_params(("arbitrary",)),
        name="pool_mixer",
    )(x2, mod, norm_g, pool_w, pool_scale)


def _ffn_kernel(x_ref, mod_ref, g_ref, w1_ref, w3_ref, w2_ref, fg_ref, o_ref, h_ref, *, final_norm):
    j = pl.program_id(1)
    last = pl.num_programs(1) - 1

    @pl.when(j == 0)
    def _():
        m = mod_ref[...]
        h = _modulated_rmsnorm(x_ref[...], g_ref[...], m[3:4], m[4:5])
        h_ref[...] = h.astype(BF16)
        o_ref[...] = jnp.zeros(o_ref.shape, F32)

    h = h_ref[...]
    a = _bdot(h, w1_ref[...])
    b = _bdot(h, w3_ref[...])
    u = (a * jax.nn.sigmoid(a) * b).astype(BF16)
    o_ref[...] += _bdot(u, w2_ref[...])

    @pl.when(j == last)
    def _():
        out = x_ref[...] + mod_ref[5:6, :] * o_ref[...]
        if final_norm:
            ms = jnp.mean(out * out, axis=-1, keepdims=True)
            out = out * lax.rsqrt(ms + EPS) * fg_ref[...]
        o_ref[...] = out


def _ffn_layer(x2, mod, norm_g, w1, w3, w2, final_g, l, final_norm, tm=512, tf=512):
    s, d = x2.shape
    d_ff = w1.shape[2]
    return pl.pallas_call(
        functools.partial(_ffn_kernel, final_norm=final_norm),
        grid=(s // tm, d_ff // tf),
        in_specs=[
            pl.BlockSpec((tm, d), lambda t, j: (t, 0)),
            pl.BlockSpec((None, N_MOD, d), lambda t, j: (l, 0, 0)),
            pl.BlockSpec((None, 1, d), lambda t, j: (l, 0, 0)),
            pl.BlockSpec((None, d, tf), lambda t, j: (l, 0, j)),
            pl.BlockSpec((None, d, tf), lambda t, j: (l, 0, j)),
            pl.BlockSpec((None, tf, d), lambda t, j: (l, j, 0)),
            pl.BlockSpec((1, d), lambda t, j: (0, 0)),
        ],
        out_specs=pl.BlockSpec((tm, d), lambda t, j: (t, 0)),
        out_shape=jax.ShapeDtypeStruct((s, d), F32),
        scratch_shapes=[pltpu.VMEM((tm, d), BF16)],
        compiler_params=_params(("arbitrary", "arbitrary")),
        name="ffn",
    )(x2, mod, norm_g, w1, w3, w2, final_g)


def kernel(x, c, ada_w, ada_b, norm_mix_g, norm_ffn_g, ab_w_in, a_conv_w, a_conv_b, a_norm_g, a_norm_b,
           b_norm_g, b_norm_b, b_w_s, b_bias, ab_w_out, pool_w, pool_scale, ffn_w1, ffn_w3, ffn_w2, final_g):
    bsz, s, d = x.shape
    depth = ada_w.shape[0]
    assert bsz == 1, "per-sequence adaLN terms are passed as single rows"
    row3 = lambda p: p.reshape(p.shape[0], 1, p.shape[1])

    mod = _adaln(c, ada_w, ada_b)
    n_even, n_heads, chunk = b_bias.shape
    bias_full = jnp.broadcast_to(
        jnp.transpose(b_bias, (0, 2, 1))[:, :, :, None], (n_even, chunk, n_heads, HEAD_DIM_B)
    ).reshape(n_even, chunk, n_heads * HEAD_DIM_B)

    w_in, w_out, p_w = ab_w_in.astype(BF16), ab_w_out.astype(BF16), pool_w.astype(BF16)
    w1, w3, w2 = ffn_w1.astype(BF16), ffn_w3.astype(BF16), ffn_w2.astype(BF16)
    mix_g, ffn_g = row3(norm_mix_g), row3(norm_ffn_g)
    conv_b, a_g, a_b, v_g, v_b = (row3(p) for p in (a_conv_b, a_norm_g, a_norm_b, b_norm_g, b_norm_b))
    p_scale = row3(pool_scale)
    fg = final_g.reshape(1, d)

    x2 = x.reshape(s, d)
    for l in range(depth):
        if l % 2 == 0:
            x2 = _even_layer(x2, mod, mix_g, w_in, a_conv_w, conv_b, a_g, a_b, v_g, v_b, b_w_s, bias_full,
                             w_out, l)
        else:
            x2 = _pool_layer(x2, mod, mix_g, p_w, p_scale, l)
        x2 = _ffn_layer(x2, mod, ffn_g, w1, w3, w2, fg, l, final_norm=(l == depth - 1))
    return x2.reshape(bsz, s, d)
```

```python
import functools

import jax
import jax.numpy as jnp
from jax import lax
from jax.experimental import pallas as pl
from jax.experimental.pallas import tpu as pltpu

F32 = jnp.float32
BF16 = jnp.bfloat16

EPS = 1e-6
CONV_WIDTH = 31
CHUNK = 128
HEAD_DIM_B = 128
POOL_WINDOWS = (2, 4, 8, 16)
N_MOD = 6

HALO = 32
SUBLANES = 8
BF16_SUBLANES = 16
LANES = 128
VMEM_LIMIT_V7X = 56 * 1024 * 1024


def _params(semantics):
    return pltpu.CompilerParams(dimension_semantics=semantics, vmem_limit_bytes=VMEM_LIMIT_V7X)


def _resident(shape, index_map):
    return pl.BlockSpec(shape, index_map, pipeline_mode=pl.Buffered(1))


def _bdot(a, b):
    return jnp.dot(a, b, preferred_element_type=F32)


def _modulated_rmsnorm(x, gain, shift, scale):
    ms = jnp.mean(x * x, axis=-1, keepdims=True)
    return x * lax.rsqrt(ms + EPS) * (gain * (1.0 + scale)) + shift


def _layernorm(x, g, b):
    mu = jnp.mean(x, axis=-1, keepdims=True)
    xc = x - mu
    var = jnp.mean(xc * xc, axis=-1, keepdims=True)
    return xc * lax.rsqrt(var + EPS) * g + b


def _adaln_kernel(c_ref, w_ref, b_ref, o_ref):
    c = c_ref[...]
    cond = c * jax.nn.sigmoid(c)
    cond8 = jnp.broadcast_to(cond, (8, cond.shape[1])).astype(BF16)
    y = _bdot(cond8, w_ref[...].astype(BF16))
    o_ref[...] = y[0:1, :] + b_ref[...]


def _adaln(c, ada_w, ada_b, tn=1024):
    depth, d, n = ada_w.shape
    out = pl.pallas_call(
        _adaln_kernel,
        grid=(depth, n // tn),
        in_specs=[
            pl.BlockSpec((1, d), lambda l, j: (0, 0)),
            pl.BlockSpec((None, d, tn), lambda l, j: (l, 0, j)),
            pl.BlockSpec((None, 1, tn), lambda l, j: (l, 0, j)),
        ],
        out_specs=pl.BlockSpec((None, 1, tn), lambda l, j: (l, 0, j)),
        out_shape=jax.ShapeDtypeStruct((depth, 1, n), F32),
        compiler_params=_params(("arbitrary", "arbitrary")),
        name="adaln",
    )(c, ada_w, ada_b.reshape(depth, 1, n))
    return out.reshape(depth, N_MOD, d)


def _even_kernel(x_ref, mod_ref, g_ref, win_ref, cw_ref, cb_ref, ag_ref, ab_ref, vg_ref, vb_ref,
                 ws_ref, bias_ref, wout_ref, o_ref, abuf_ref):
    tm = x_ref.shape[0]
    d_a = cw_ref.shape[1]
    n_heads = ws_ref.shape[0]
    n_chunks = tm // CHUNK

    @pl.when(pl.program_id(0) == 0)
    def _():
        abuf_ref[0:HALO, :] = jnp.zeros((HALO, d_a), F32)
        abuf_ref[HALO + tm:HALO + tm + SUBLANES, :] = jnp.zeros((SUBLANES, d_a), F32)

    x = x_ref[...]
    m = mod_ref[...]
    shift, scale, gate = m[0:1], m[1:2], m[2:3]
    h = _modulated_rmsnorm(x, g_ref[...], shift, scale)
    proj = _bdot(h.astype(BF16), win_ref[...])

    a = proj[:, 0:d_a] * jax.nn.sigmoid(proj[:, d_a:2 * d_a])
    abuf_ref[HALO:HALO + tm, :] = a
    first = HALO - (CONV_WIDTH - 1)
    acc = cb_ref[...]
    for r in range(SUBLANES):
        p_r = None
        for q in range((first + CONV_WIDTH - 1) // SUBLANES + 1):
            k = SUBLANES * q + r - first
            if 0 <= k < CONV_WIDTH:
                term = cw_ref[k:k + 1, :] * abuf_ref[SUBLANES * q:SUBLANES * q + tm + SUBLANES, :]
                p_r = term if p_r is None else p_r + term
        acc = acc + p_r[r:r + tm, :]
    abuf_ref[0:HALO, :] = abuf_ref[tm:tm + HALO, :]
    a_ln = _layernorm(acc, ag_ref[...], ab_ref[...])
    a_out = a_ln * jax.nn.sigmoid(a_ln)

    b_u = proj[:, 2 * d_a:2 * d_a + n_heads * HEAD_DIM_B]
    b_v = proj[:, 2 * d_a + n_heads * HEAD_DIM_B:]
    v = _layernorm(b_v, vg_ref[...], vb_ref[...]).astype(BF16)
    row = lax.broadcasted_iota(jnp.int32, (CHUNK, CHUNK), 0)
    col = lax.broadcasted_iota(jnp.int32, (CHUNK, CHUNK), 1)
    causal = row >= col
    mixed = []
    for hd in range(n_heads):
        lanes = slice(hd * HEAD_DIM_B, (hd + 1) * HEAD_DIM_B)
        rhs = jnp.concatenate([v[c * CHUNK:(c + 1) * CHUNK, lanes] for c in range(n_chunks)], axis=1)
        w_c = jnp.where(causal, ws_ref[hd], 0.0).astype(BF16)
        mixed.append(_bdot(w_c, rhs))
    bias = bias_ref[...]
    rows = []
    for c in range(n_chunks):
        lanes = slice(c * HEAD_DIM_B, (c + 1) * HEAD_DIM_B)
        rows.append(jnp.concatenate([mixed[hd][:, lanes] for hd in range(n_heads)], axis=1) + bias)
    b_out = b_u * jnp.concatenate(rows, axis=0)

    cat = jnp.concatenate([a_out, b_out], axis=1).astype(BF16)
    y = _bdot(cat, wout_ref[...])
    o_ref[...] = x + gate * y


def _even_layer(x2, mod, norm_g, w_in, conv_w, conv_b, a_g, a_b, v_g, v_b, w_s, bias_full, w_out, l, tm=256):
    s, d = x2.shape
    i = l // 2
    d_in = w_in.shape[1]
    d_a = conv_w.shape[2]
    d_b = v_g.shape[2]
    n_heads = w_s.shape[1]
    vec = lambda width, idx: pl.BlockSpec((None, 1, width), lambda t: (idx, 0, 0))
    return pl.pallas_call(
        _even_kernel,
        grid=(s // tm,),
        in_specs=[
            pl.BlockSpec((tm, d), lambda t: (t, 0)),
            pl.BlockSpec((None, N_MOD, d), lambda t: (l, 0, 0)),
            vec(d, l),
            _resident((d, d_in), lambda t: (0, 0)),
            pl.BlockSpec((None, CONV_WIDTH, d_a), lambda t: (i, 0, 0)),
            vec(d_a, i), vec(d_a, i), vec(d_a, i), vec(d_b, i), vec(d_b, i),
            pl.BlockSpec((None, n_heads, CHUNK, CHUNK), lambda t: (i, 0, 0, 0)),
            pl.BlockSpec((None, CHUNK, d_b), lambda t: (i, 0, 0)),
            _resident((d_a + d_b, d), lambda t: (0, 0)),
        ],
        out_specs=pl.BlockSpec((tm, d), lambda t: (t, 0)),
        out_shape=jax.ShapeDtypeStruct((s, d), F32),
        scratch_shapes=[pltpu.VMEM((HALO + tm + SUBLANES, d_a), F32)],
        compiler_params=_params(("arbitrary",)),
        name="even_mixer",
    )(x2, mod, norm_g, w_in, conv_w, conv_b, a_g, a_b, v_g, v_b, w_s, bias_full, w_out)


def _pool_kernel(x_ref, mod_ref, g_ref, pw_ref, ps_ref, o_ref, hbuf_ref, p_ref, q_ref):
    tm, d = x_ref.shape
    n_groups = len(POOL_WINDOWS)
    dg = d // n_groups
    t0 = pl.program_id(0) * tm

    @pl.when(pl.program_id(0) == 0)
    def _():
        hbuf_ref[0:HALO, :] = jnp.zeros((HALO, d), F32)

    x = x_ref[...]
    m = mod_ref[...]
    shift, scale, gate = m[0:1], m[1:2], m[2:3]
    h = _modulated_rmsnorm(x, g_ref[...], shift, scale)
    hbuf_ref[HALO:HALO + tm, :] = h

    end = HALO + tm
    p_ref[8:end, :] = hbuf_ref[8:end, :] + hbuf_ref[7:end - 1, :]
    q_ref[16:end, dg:] = p_ref[16:end, dg:] + p_ref[14:end - 2, dg:]
    p_ref[24:end, 2 * dg:] = q_ref[24:end, 2 * dg:] + q_ref[20:end - 4, 2 * dg:]
    q_ref[32:end, 3 * dg:] = p_ref[32:end, 3 * dg:] + p_ref[24:end - 8, 3 * dg:]
    sums = (p_ref, q_ref, p_ref, q_ref)

    pos = (lax.broadcasted_iota(jnp.int32, (tm, dg), 0) + (t0 + 1)).astype(F32)
    ys = []
    for gi, w in enumerate(POOL_WINDOWS):
        lanes = slice(gi * dg, (gi + 1) * dg)
        cnt = jnp.minimum(pos, float(w))
        pooled = sums[gi][HALO:end, lanes] / cnt - h[:, lanes]
        ys.append(_bdot(pooled.astype(BF16), pw_ref[gi]))
    y = jnp.concatenate(ys, axis=1) * ps_ref[...]
    o_ref[...] = x + gate * y
    hbuf_ref[0:HALO, :] = hbuf_ref[tm:tm + HALO, :]


def _pool_layer(x2, mod, norm_g, pool_w, pool_scale, l, tm=512):
    s, d = x2.shape
    i = l // 2
    n_groups, dg, _ = pool_w.shape[1:]
    return pl.pallas_call(
        _pool_kernel,
        grid=(s // tm,),
        in_specs=[
            pl.BlockSpec((tm, d), lambda t: (t, 0)),
            pl.BlockSpec((None, N_MOD, d), lambda t: (l, 0, 0)),
            pl.BlockSpec((None, 1, d), lambda t: (l, 0, 0)),
            _resident((None, n_groups, dg, dg), lambda t: (i, 0, 0, 0)),
            pl.BlockSpec((None, 1, d), lambda t: (i, 0, 0)),
        ],
        out_specs=pl.BlockSpec((tm, d), lambda t: (t, 0)),
        out_shape=jax.ShapeDtypeStruct((s, d), F32),
        scratch_shapes=[pltpu.VMEM((HALO + tm, d), F32)] * 3,
        compiler_params=_params(("arbitrary",)),
        name="pool_mixer",
    )(x2, mod, norm_g, pool_w, pool_scale)


N_FFN_IN = 8


def _ffn_kernel(*refs, final_norm, n_cast, h_rows):
    x_ref, xn_ref, mod_ref, g_ref, w1_ref, w3_ref, w2_ref, fg_ref = refs[:N_FFN_IN]
    cast_src = refs[N_FFN_IN:N_FFN_IN + n_cast]
    o_ref = refs[N_FFN_IN + n_cast]
    cast_dst = refs[N_FFN_IN + n_cast + 1:N_FFN_IN + 2 * n_cast + 1]
    h_ref, hn_ref = refs[N_FFN_IN + 2 * n_cast + 1:]
    i, j = pl.program_id(0), pl.program_id(1)
    last = pl.num_programs(1) - 1
    tm = x_ref.shape[0]
    m = mod_ref[...]
    gain, shift, scale = g_ref[...], m[3:4], m[4:5]

    @pl.when((j == 0) & (i == 0))
    def _():
        h_ref[...] = _modulated_rmsnorm(x_ref[...], gain, shift, scale).astype(BF16)

    @pl.when((j == 0) & (i > 0))
    def _():
        h_ref[...] = hn_ref[...]

    @pl.when(j == 0)
    def _():
        o_ref[...] = jnp.zeros(o_ref.shape, F32)

    r0 = pl.multiple_of(jnp.minimum(j * h_rows, tm - h_rows), BF16_SUBLANES)
    xn = xn_ref[pl.ds(r0, h_rows), :]
    hn_ref[pl.ds(r0, h_rows), :] = _modulated_rmsnorm(xn, gain, shift, scale).astype(BF16)
    for src, dst in zip(cast_src, cast_dst):
        dst[...] = src[...].astype(BF16)

    h = h_ref[...]
    a = _bdot(h, w1_ref[...])
    b = _bdot(h, w3_ref[...])
    u = (a * jax.nn.sigmoid(a) * b).astype(BF16)
    o_ref[...] += _bdot(u, w2_ref[...])

    @pl.when(j == last)
    def _():
        out = x_ref[...] + m[5:6] * o_ref[...]
        if final_norm:
            ms = jnp.mean(out * out, axis=-1, keepdims=True)
            out = out * lax.rsqrt(ms + EPS) * fg_ref[...]
        o_ref[...] = out


def _cast_job(w, layer, n_i, n_j):
    rows, cols = w.shape[1:]
    if rows % (n_i * n_j * BF16_SUBLANES) == 0:
        rb = rows // (n_i * n_j)
        in_spec = pl.BlockSpec((None, rb, cols), lambda t, j: (layer, j * n_i + t, 0))
        out_spec = pl.BlockSpec((rb, cols), lambda t, j: (j * n_i + t, 0))
    else:
        n_cb = max(k for k in range(1, n_j + 1) if cols % (k * LANES) == 0)
        rb, cb = rows // n_i, cols // n_cb
        in_spec = pl.BlockSpec((None, rb, cb), lambda t, j: (layer, t, jnp.minimum(j, n_cb - 1)))
        out_spec = pl.BlockSpec((rb, cb), lambda t, j: (t, jnp.minimum(j, n_cb - 1)))
    return w, in_spec, out_spec, jax.ShapeDtypeStruct((rows, cols), BF16)


def _ffn_layer(x2, mod, norm_g, w1, w3, w2, final_g, l, final_norm, cast_ahead=(), tm=512, tf=512):
    s, d = x2.shape
    d_ff = w1.shape[1]
    n_i, n_j = s // tm, d_ff // tf
    h_rows = -(-tm // (n_j * BF16_SUBLANES)) * BF16_SUBLANES
    jobs = [_cast_job(w, layer, n_i, n_j) for w, layer in cast_ahead]
    outs = pl.pallas_call(
        functools.partial(_ffn_kernel, final_norm=final_norm, n_cast=len(jobs), h_rows=h_rows),
        grid=(n_i, n_j),
        in_specs=[
            pl.BlockSpec((tm, d), lambda t, j: (t, 0)),
            pl.BlockSpec((tm, d), lambda t, j: (jnp.minimum(t + 1, n_i - 1), 0)),
            pl.BlockSpec((None, N_MOD, d), lambda t, j: (l, 0, 0)),
            pl.BlockSpec((None, 1, d), lambda t, j: (l, 0, 0)),
            pl.BlockSpec((d, tf), lambda t, j: (0, j)),
            pl.BlockSpec((d, tf), lambda t, j: (0, j)),
            pl.BlockSpec((tf, d), lambda t, j: (j, 0)),
            pl.BlockSpec((1, d), lambda t, j: (0, 0)),
        ] + [job[1] for job in jobs],
        out_specs=[pl.BlockSpec((tm, d), lambda t, j: (t, 0))] + [job[2] for job in jobs],
        out_shape=[jax.ShapeDtypeStruct((s, d), F32)] + [job[3] for job in jobs],
        scratch_shapes=[pltpu.VMEM((tm, d), BF16), pltpu.VMEM((tm, d), BF16)],
        compiler_params=_params(("arbitrary", "arbitrary")),
        name="ffn",
    )(x2, x2, mod, norm_g, w1, w3, w2, final_g, *[job[0] for job in jobs])
    return outs[0], outs[1:]


def kernel(x, c, ada_w, ada_b, norm_mix_g, norm_ffn_g, ab_w_in, a_conv_w, a_conv_b, a_norm_g, a_norm_b,
           b_norm_g, b_norm_b, b_w_s, b_bias, ab_w_out, pool_w, pool_scale, ffn_w1, ffn_w3, ffn_w2, final_g):
    bsz, s, d = x.shape
    depth = ada_w.shape[0]
    assert bsz == 1, "per-sequence adaLN terms are passed as single rows"
    row3 = lambda p: p.reshape(p.shape[0], 1, p.shape[1])

    mod = _adaln(c, ada_w, ada_b)
    n_even, n_heads, chunk = b_bias.shape
    bias_full = jnp.broadcast_to(
        jnp.transpose(b_bias, (0, 2, 1))[:, :, :, None], (n_even, chunk, n_heads, HEAD_DIM_B)
    ).reshape(n_even, chunk, n_heads * HEAD_DIM_B)

    p_w = pool_w.astype(BF16)
    w_in, w_out = ab_w_in[0].astype(BF16), ab_w_out[0].astype(BF16)
    w1, w3, w2 = ffn_w1[0].astype(BF16), ffn_w3[0].astype(BF16), ffn_w2[0].astype(BF16)
    mix_g, ffn_g = row3(norm_mix_g), row3(norm_ffn_g)
    conv_b, a_g, a_b, v_g, v_b = (row3(p) for p in (a_conv_b, a_norm_g, a_norm_b, b_norm_g, b_norm_b))
    p_scale = row3(pool_scale)
    fg = final_g.reshape(1, d)

    x2 = x.reshape(s, d)
    for l in range(depth):
        if l % 2 == 0:
            x2 = _even_layer(x2, mod, mix_g, w_in, a_conv_w, conv_b, a_g, a_b, v_g, v_b, b_w_s, bias_full,
                             w_out, l)
        else:
            x2 = _pool_layer(x2, mod, mix_g, p_w, p_scale, l)
        cast_ahead = []
        if l + 1 < depth:
            cast_ahead += [(ffn_w1, l + 1), (ffn_w3, l + 1), (ffn_w2, l + 1)]
            if (l + 1) % 2 == 0:
                cast_ahead += [(ab_w_in, (l + 1) // 2), (ab_w_out, (l + 1) // 2)]
        x2, casts = _ffn_layer(x2, mod, ffn_g, w1, w3, w2, fg, l, final_norm=(l == depth - 1),
                               cast_ahead=cast_ahead)
        if casts:
            w1, w3, w2 = casts[:3]
            if len(casts) > 3:
                w_in, w_out = casts[3:]
    return x2.reshape(bsz, s, d)
```

```python
import functools

import jax
import jax.numpy as jnp
from jax import lax
from jax.experimental import pallas as pl
from jax.experimental.pallas import tpu as pltpu

F32 = jnp.float32
BF16 = jnp.bfloat16

EPS = 1e-6
CONV_WIDTH = 31
CHUNK = 128
HEAD_DIM_B = 128
POOL_WINDOWS = (2, 4, 8, 16)
N_MOD = 6

HALO = 32
SUBLANES = 8
BF16_SUBLANES = 16
LANES = 128
VMEM_LIMIT_V7X = 56 * 1024 * 1024


def _params(semantics):
    return pltpu.CompilerParams(dimension_semantics=semantics, vmem_limit_bytes=VMEM_LIMIT_V7X)


def _resident(shape, index_map):
    return pl.BlockSpec(shape, index_map, pipeline_mode=pl.Buffered(1))


def _bdot(a, b):
    return jnp.dot(a, b, preferred_element_type=F32)


def _modulated_rmsnorm(x, gain, shift, scale):
    ms = jnp.mean(x * x, axis=-1, keepdims=True)
    return x * lax.rsqrt(ms + EPS) * (gain * (1.0 + scale)) + shift


def _layernorm(x, g, b):
    mu = jnp.mean(x, axis=-1, keepdims=True)
    xc = x - mu
    var = jnp.mean(xc * xc, axis=-1, keepdims=True)
    return xc * lax.rsqrt(var + EPS) * g + b


def _store_residual_and_ffn_input(x_new, m, fg_ref, o_ref, h2_ref):
    o_ref[...] = x_new
    h2_ref[...] = _modulated_rmsnorm(x_new, fg_ref[...], m[3:4], m[4:5]).astype(BF16)


def _adaln_kernel(c_ref, w_ref, b_ref, o_ref):
    c = c_ref[...]
    cond = c * jax.nn.sigmoid(c)
    cond8 = jnp.broadcast_to(cond, (8, cond.shape[1])).astype(BF16)
    y = _bdot(cond8, w_ref[...].astype(BF16))
    o_ref[...] = y[0:1, :] + b_ref[...]


def _adaln(c, ada_w, ada_b, tn=1024):
    depth, d, n = ada_w.shape
    out = pl.pallas_call(
        _adaln_kernel,
        grid=(depth, n // tn),
        in_specs=[
            pl.BlockSpec((1, d), lambda l, j: (0, 0)),
            pl.BlockSpec((None, d, tn), lambda l, j: (l, 0, j)),
            pl.BlockSpec((None, 1, tn), lambda l, j: (l, 0, j)),
        ],
        out_specs=pl.BlockSpec((None, 1, tn), lambda l, j: (l, 0, j)),
        out_shape=jax.ShapeDtypeStruct((depth, 1, n), F32),
        compiler_params=_params(("arbitrary", "arbitrary")),
        name="adaln",
    )(c, ada_w, ada_b.reshape(depth, 1, n))
    return out.reshape(depth, N_MOD, d)


def _even_kernel(x_ref, mod_ref, g_ref, fg_ref, win_ref, cw_ref, cb_ref, ag_ref, ab_ref, vg_ref, vb_ref,
                 ws_ref, bias_ref, wout_ref, o_ref, h2_ref, abuf_ref):
    tm = x_ref.shape[0]
    d_a = cw_ref.shape[1]
    n_heads = ws_ref.shape[0]
    n_chunks = tm // CHUNK

    @pl.when(pl.program_id(0) == 0)
    def _():
        abuf_ref[0:HALO, :] = jnp.zeros((HALO, d_a), F32)
        abuf_ref[HALO + tm:HALO + tm + SUBLANES, :] = jnp.zeros((SUBLANES, d_a), F32)

    x = x_ref[...]
    m = mod_ref[...]
    shift, scale, gate = m[0:1], m[1:2], m[2:3]
    h = _modulated_rmsnorm(x, g_ref[...], shift, scale)
    proj = _bdot(h.astype(BF16), win_ref[...])

    a = proj[:, 0:d_a] * jax.nn.sigmoid(proj[:, d_a:2 * d_a])
    abuf_ref[HALO:HALO + tm, :] = a
    first = HALO - (CONV_WIDTH - 1)
    acc = cb_ref[...]
    for r in range(SUBLANES):
        p_r = None
        for q in range((first + CONV_WIDTH - 1) // SUBLANES + 1):
            k = SUBLANES * q + r - first
            if 0 <= k < CONV_WIDTH:
                term = cw_ref[k:k + 1, :] * abuf_ref[SUBLANES * q:SUBLANES * q + tm + SUBLANES, :]
                p_r = term if p_r is None else p_r + term
        acc = acc + p_r[r:r + tm, :]
    abuf_ref[0:HALO, :] = abuf_ref[tm:tm + HALO, :]
    a_ln = _layernorm(acc, ag_ref[...], ab_ref[...])
    a_out = a_ln * jax.nn.sigmoid(a_ln)

    b_u = proj[:, 2 * d_a:2 * d_a + n_heads * HEAD_DIM_B]
    b_v = proj[:, 2 * d_a + n_heads * HEAD_DIM_B:]
    v = _layernorm(b_v, vg_ref[...], vb_ref[...]).astype(BF16)
    row = lax.broadcasted_iota(jnp.int32, (CHUNK, CHUNK), 0)
    col = lax.broadcasted_iota(jnp.int32, (CHUNK, CHUNK), 1)
    causal = row >= col
    mixed = []
    for hd in range(n_heads):
        lanes = slice(hd * HEAD_DIM_B, (hd + 1) * HEAD_DIM_B)
        rhs = jnp.concatenate([v[c * CHUNK:(c + 1) * CHUNK, lanes] for c in range(n_chunks)], axis=1)
        w_c = jnp.where(causal, ws_ref[hd], 0.0).astype(BF16)
        mixed.append(_bdot(w_c, rhs))
    bias = bias_ref[...]
    rows = []
    for c in range(n_chunks):
        lanes = slice(c * HEAD_DIM_B, (c + 1) * HEAD_DIM_B)
        rows.append(jnp.concatenate([mixed[hd][:, lanes] for hd in range(n_heads)], axis=1) + bias)
    b_out = b_u * jnp.concatenate(rows, axis=0)

    cat = jnp.concatenate([a_out, b_out], axis=1).astype(BF16)
    y = _bdot(cat, wout_ref[...])
    _store_residual_and_ffn_input(x + gate * y, m, fg_ref, o_ref, h2_ref)


def _even_layer(x2, mod, norm_g, ffn_g, w_in, conv_w, conv_b, a_g, a_b, v_g, v_b, w_s, bias_full, w_out, l,
                tm=256):
    s, d = x2.shape
    i = l // 2
    d_in = w_in.shape[1]
    d_a = conv_w.shape[2]
    d_b = v_g.shape[2]
    n_heads = w_s.shape[1]
    vec = lambda width, idx: pl.BlockSpec((None, 1, width), lambda t: (idx, 0, 0))
    return pl.pallas_call(
        _even_kernel,
        grid=(s // tm,),
        in_specs=[
            pl.BlockSpec((tm, d), lambda t: (t, 0)),
            pl.BlockSpec((None, N_MOD, d), lambda t: (l, 0, 0)),
            vec(d, l),
            vec(d, l),
            _resident((d, d_in), lambda t: (0, 0)),
            pl.BlockSpec((None, CONV_WIDTH, d_a), lambda t: (i, 0, 0)),
            vec(d_a, i), vec(d_a, i), vec(d_a, i), vec(d_b, i), vec(d_b, i),
            pl.BlockSpec((None, n_heads, CHUNK, CHUNK), lambda t: (i, 0, 0, 0)),
            pl.BlockSpec((None, CHUNK, d_b), lambda t: (i, 0, 0)),
            _resident((d_a + d_b, d), lambda t: (0, 0)),
        ],
        out_specs=[pl.BlockSpec((tm, d), lambda t: (t, 0))] * 2,
        out_shape=[jax.ShapeDtypeStruct((s, d), F32), jax.ShapeDtypeStruct((s, d), BF16)],
        scratch_shapes=[pltpu.VMEM((HALO + tm + SUBLANES, d_a), F32)],
        compiler_params=_params(("arbitrary",)),
        name="even_mixer",
    )(x2, mod, norm_g, ffn_g, w_in, conv_w, conv_b, a_g, a_b, v_g, v_b, w_s, bias_full, w_out)


def _pool_kernel(x_ref, mod_ref, g_ref, fg_ref, pw_ref, ps_ref, o_ref, h2_ref, hbuf_ref, p_ref, q_ref):
    tm, d = x_ref.shape
    n_groups = len(POOL_WINDOWS)
    dg = d // n_groups
    t0 = pl.program_id(0) * tm

    @pl.when(pl.program_id(0) == 0)
    def _():
        hbuf_ref[0:HALO, :] = jnp.zeros((HALO, d), F32)

    x = x_ref[...]
    m = mod_ref[...]
    shift, scale, gate = m[0:1], m[1:2], m[2:3]
    h = _modulated_rmsnorm(x, g_ref[...], shift, scale)
    hbuf_ref[HALO:HALO + tm, :] = h

    end = HALO + tm
    p_ref[8:end, :] = hbuf_ref[8:end, :] + hbuf_ref[7:end - 1, :]
    q_ref[16:end, dg:] = p_ref[16:end, dg:] + p_ref[14:end - 2, dg:]
    p_ref[24:end, 2 * dg:] = q_ref[24:end, 2 * dg:] + q_ref[20:end - 4, 2 * dg:]
    q_ref[32:end, 3 * dg:] = p_ref[32:end, 3 * dg:] + p_ref[24:end - 8, 3 * dg:]
    sums = (p_ref, q_ref, p_ref, q_ref)

    pos = (lax.broadcasted_iota(jnp.int32, (tm, dg), 0) + (t0 + 1)).astype(F32)
    ys = []
    for gi, w in enumerate(POOL_WINDOWS):
        lanes = slice(gi * dg, (gi + 1) * dg)
        cnt = jnp.minimum(pos, float(w))
        pooled = sums[gi][HALO:end, lanes] / cnt - h[:, lanes]
        ys.append(_bdot(pooled.astype(BF16), pw_ref[gi]))
    y = jnp.concatenate(ys, axis=1) * ps_ref[...]
    _store_residual_and_ffn_input(x + gate * y, m, fg_ref, o_ref, h2_ref)
    hbuf_ref[0:HALO, :] = hbuf_ref[tm:tm + HALO, :]


def _pool_layer(x2, mod, norm_g, ffn_g, pool_w, pool_scale, l, tm=512):
    s, d = x2.shape
    i = l // 2
    n_groups, dg, _ = pool_w.shape[1:]
    return pl.pallas_call(
        _pool_kernel,
        grid=(s // tm,),
        in_specs=[
            pl.BlockSpec((tm, d), lambda t: (t, 0)),
            pl.BlockSpec((None, N_MOD, d), lambda t: (l, 0, 0)),
            pl.BlockSpec((None, 1, d), lambda t: (l, 0, 0)),
            pl.BlockSpec((None, 1, d), lambda t: (l, 0, 0)),
            _resident((None, n_groups, dg, dg), lambda t: (i, 0, 0, 0)),
            pl.BlockSpec((None, 1, d), lambda t: (i, 0, 0)),
        ],
        out_specs=[pl.BlockSpec((tm, d), lambda t: (t, 0))] * 2,
        out_shape=[jax.ShapeDtypeStruct((s, d), F32), jax.ShapeDtypeStruct((s, d), BF16)],
        scratch_shapes=[pltpu.VMEM((HALO + tm, d), F32)] * 3,
        compiler_params=_params(("arbitrary",)),
        name="pool_mixer",
    )(x2, mod, norm_g, ffn_g, pool_w, pool_scale)


N_FFN_IN = 7


def _ffn_kernel(*refs, final_norm, n_cast, n_x):
    h_ref, x_ref, mod_ref, w1_ref, w3_ref, w2_ref, fg_ref = refs[:N_FFN_IN]
    cast_src = refs[N_FFN_IN:N_FFN_IN + n_cast]
    o_ref = refs[N_FFN_IN + n_cast]
    cast_dst = refs[N_FFN_IN + n_cast + 1:]
    j = pl.program_id(1)
    last = pl.num_programs(1) - 1
    x_rows = x_ref.shape[0]

    @pl.when(j == 0)
    def _():
        o_ref[...] = jnp.zeros(o_ref.shape, F32)

    r0 = pl.multiple_of(jnp.minimum(j, n_x - 1) * x_rows, x_rows)
    o_ref[pl.ds(r0, x_rows), :] += jnp.where(j < n_x, x_ref[...], 0.0)
    for src, dst in zip(cast_src, cast_dst):
        dst[...] = src[...].astype(BF16)

    h = h_ref[...]
    a = _bdot(h, w1_ref[...])
    b = _bdot(h, w3_ref[...])
    u = (a * jax.nn.sigmoid(a) * b).astype(BF16)
    o_ref[...] += mod_ref[5:6, :] * _bdot(u, w2_ref[...])

    if final_norm:
        @pl.when(j == last)
        def _():
            out = o_ref[...]
            ms = jnp.mean(out * out, axis=-1, keepdims=True)
            o_ref[...] = out * lax.rsqrt(ms + EPS) * fg_ref[...]


def _cast_job(w, layer, n_i, n_j):
    rows, cols = w.shape[1:]
    if rows % (n_i * n_j * BF16_SUBLANES) == 0:
        rb = rows // (n_i * n_j)
        in_spec = pl.BlockSpec((None, rb, cols), lambda t, j: (layer, j * n_i + t, 0))
        out_spec = pl.BlockSpec((rb, cols), lambda t, j: (j * n_i + t, 0))
    else:
        n_cb = max(k for k in range(1, n_j + 1) if cols % (k * LANES) == 0)
        rb, cb = rows // n_i, cols // n_cb
        in_spec = pl.BlockSpec((None, rb, cb), lambda t, j: (layer, t, jnp.minimum(j, n_cb - 1)))
        out_spec = pl.BlockSpec((rb, cb), lambda t, j: (t, jnp.minimum(j, n_cb - 1)))
    return w, in_spec, out_spec, jax.ShapeDtypeStruct((rows, cols), BF16)


def _ffn_layer(h2, x2, mod, w1, w3, w2, final_g, l, final_norm, cast_ahead=(), tm=1024, tf=512):
    s, d = x2.shape
    d_ff = w1.shape[1]
    n_i, n_j = s // tm, d_ff // tf
    n_x = max(k for k in range(1, n_j + 1) if tm % (k * SUBLANES) == 0)
    jobs = [_cast_job(w, layer, n_i, n_j) for w, layer in cast_ahead]
    outs = pl.pallas_call(
        functools.partial(_ffn_kernel, final_norm=final_norm, n_cast=len(jobs), n_x=n_x),
        grid=(n_i, n_j),
        in_specs=[
            pl.BlockSpec((tm, d), lambda t, j: (t, 0)),
            pl.BlockSpec((tm // n_x, d), lambda t, j: (t * n_x + jnp.minimum(j, n_x - 1), 0)),
            pl.BlockSpec((None, N_MOD, d), lambda t, j: (l, 0, 0)),
            pl.BlockSpec((d, tf), lambda t, j: (0, j)),
            pl.BlockSpec((d, tf), lambda t, j: (0, j)),
            pl.BlockSpec((tf, d), lambda t, j: (j, 0)),
            pl.BlockSpec((1, d), lambda t, j: (0, 0)),
        ] + [job[1] for job in jobs],
        out_specs=[pl.BlockSpec((tm, d), lambda t, j: (t, 0))] + [job[2] for job in jobs],
        out_shape=[jax.ShapeDtypeStruct((s, d), F32)] + [job[3] for job in jobs],
        compiler_params=_params(("arbitrary", "arbitrary")),
        name="ffn",
    )(h2, x2, mod, w1, w3, w2, final_g, *[job[0] for job in jobs])
    return outs[0], outs[1:]


def kernel(x, c, ada_w, ada_b, norm_mix_g, norm_ffn_g, ab_w_in, a_conv_w, a_conv_b, a_norm_g, a_norm_b,
           b_norm_g, b_norm_b, b_w_s, b_bias, ab_w_out, pool_w, pool_scale, ffn_w1, ffn_w3, ffn_w2, final_g):
    bsz, s, d = x.shape
    depth = ada_w.shape[0]
    assert bsz == 1, "per-sequence adaLN terms are passed as single rows"
    row3 = lambda p: p.reshape(p.shape[0], 1, p.shape[1])

    mod = _adaln(c, ada_w, ada_b)
    n_even, n_heads, chunk = b_bias.shape
    bias_full = jnp.broadcast_to(
        jnp.transpose(b_bias, (0, 2, 1))[:, :, :, None], (n_even, chunk, n_heads, HEAD_DIM_B)
    ).reshape(n_even, chunk, n_heads * HEAD_DIM_B)

    p_w = pool_w.astype(BF16)
    w_in, w_out = ab_w_in[0].astype(BF16), ab_w_out[0].astype(BF16)
    w1, w3, w2 = ffn_w1[0].astype(BF16), ffn_w3[0].astype(BF16), ffn_w2[0].astype(BF16)
    mix_g, ffn_g = row3(norm_mix_g), row3(norm_ffn_g)
    conv_b, a_g, a_b, v_g, v_b = (row3(p) for p in (a_conv_b, a_norm_g, a_norm_b, b_norm_g, b_norm_b))
    p_scale = row3(pool_scale)
    fg = final_g.reshape(1, d)

    x2 = x.reshape(s, d)
    for l in range(depth):
        if l % 2 == 0:
            x2, h2 = _even_layer(x2, mod, mix_g, ffn_g, w_in, a_conv_w, conv_b, a_g, a_b, v_g, v_b, b_w_s,
                                 bias_full, w_out, l)
        else:
            x2, h2 = _pool_layer(x2, mod, mix_g, ffn_g, p_w, p_scale, l)
        cast_ahead = []
        if l + 1 < depth:
            cast_ahead += [(ffn_w1, l + 1), (ffn_w3, l + 1), (ffn_w2, l + 1)]
            if (l + 1) % 2 == 0:
                cast_ahead += [(ab_w_in, (l + 1) // 2), (ab_w_out, (l + 1) // 2)]
        x2, casts = _ffn_layer(h2, x2, mod, w1, w3, w2, fg, l, final_norm=(l == depth - 1),
                               cast_ahead=cast_ahead)
        if casts:
            w1, w3, w2 = casts[:3]
            if len(casts) > 3:
                w_in, w_out = casts[3:]
    return x2.reshape(bsz, s, d)
```

```python
import functools

import jax
import jax.numpy as jnp
from jax import lax
from jax.experimental import pallas as pl
from jax.experimental.pallas import tpu as pltpu

F32 = jnp.float32
BF16 = jnp.bfloat16

EPS = 1e-6
CONV_WIDTH = 31
CHUNK = 128
HEAD_DIM_B = 128
POOL_WINDOWS = (2, 4, 8, 16)
N_MOD = 6

HALO = 32
SUBLANES = 8
BF16_SUBLANES = 16
LANES = 128
W_BLOCK = 512
ADA_ROW_BLOCKS = 8
VMEM_LIMIT_V7X = 56 * 1024 * 1024


def _params(semantics):
    return pltpu.CompilerParams(dimension_semantics=semantics, vmem_limit_bytes=VMEM_LIMIT_V7X)


def _resident(shape, index_map):
    return pl.BlockSpec(shape, index_map, pipeline_mode=pl.Buffered(1))


def _bdot(a, b):
    return jnp.dot(a, b, preferred_element_type=F32)


def _modulated_rmsnorm(x, gain, shift, scale):
    ms = jnp.mean(x * x, axis=-1, keepdims=True)
    return x * lax.rsqrt(ms + EPS) * (gain * (1.0 + scale)) + shift


def _layernorm(x, g, b):
    mu = jnp.mean(x, axis=-1, keepdims=True)
    xc = x - mu
    var = jnp.mean(xc * xc, axis=-1, keepdims=True)
    return xc * lax.rsqrt(var + EPS) * g + b


def _store_residual_and_ffn_input(x_new, m, fg_ref, o_ref, h2_ref):
    o_ref[...] = x_new
    h2_ref[...] = _modulated_rmsnorm(x_new, fg_ref[...], m[3:4], m[4:5]).astype(BF16)


def _adaln_kernel(c_ref, w_ref, b_ref, o_ref):
    c = c_ref[...]
    cond = c * jax.nn.sigmoid(c)
    cond8 = jnp.broadcast_to(cond, (8, cond.shape[1])).astype(BF16)
    y = _bdot(cond8, w_ref[...].astype(BF16))
    o_ref[...] = y[0:1, :] + b_ref[...]


def _adaln_first(c, ada_w, ada_b3, tn=1024):
    _, d, n = ada_w.shape
    out = pl.pallas_call(
        _adaln_kernel,
        grid=(n // tn,),
        in_specs=[
            pl.BlockSpec((1, d), lambda j: (0, 0)),
            pl.BlockSpec((None, d, tn), lambda j: (0, 0, j)),
            pl.BlockSpec((None, 1, tn), lambda j: (0, 0, j)),
        ],
        out_specs=pl.BlockSpec((1, tn), lambda j: (0, j)),
        out_shape=jax.ShapeDtypeStruct((1, n), F32),
        compiler_params=_params(("arbitrary",)),
        name="adaln",
    )(c, ada_w, ada_b3)
    return out.reshape(N_MOD, d)


def _even_kernel(x_ref, xp_ref, mod_ref, g_ref, fg_ref, win_ref, cw_ref, cb_ref, ag_ref, ab_ref, vg_ref, vb_ref,
                 ws_ref, bias_ref, wout_ref, o_ref, h2_ref, abuf_ref, cat_ref):
    tm = x_ref.shape[0]
    d_a = cw_ref.shape[1]
    n_heads = ws_ref.shape[0]
    n_chunks = tm // CHUNK
    d_b = n_heads * HEAD_DIM_B

    @pl.when(pl.program_id(0) == 0)
    def _():
        abuf_ref[0:HALO, :] = jnp.zeros((HALO, d_a), F32)
        abuf_ref[HALO + tm:HALO + tm + SUBLANES, :] = jnp.zeros((SUBLANES, d_a), F32)
        cat_ref[...] = jnp.zeros(cat_ref.shape, BF16)

    m = mod_ref[...]
    shift, scale, gate = m[0:1], m[1:2], m[2:3]
    cat_prev = cat_ref[...]

    h = _modulated_rmsnorm(x_ref[...], g_ref[...], shift, scale).astype(BF16)
    proj = jnp.concatenate([_bdot(h, win_ref[c]) for c in range(win_ref.shape[0])], axis=1)

    abuf_ref[HALO:HALO + tm, :] = proj[:, 0:d_a] * jax.nn.sigmoid(proj[:, d_a:2 * d_a])
    first = HALO - (CONV_WIDTH - 1)

    def conv_lane_block(i):
        lanes = slice(i * LANES, (i + 1) * LANES)
        acc = cb_ref[:, lanes]
        for r in range(SUBLANES):
            p_r = None
            for q in range((first + CONV_WIDTH - 1) // SUBLANES + 1):
                k = SUBLANES * q + r - first
                if 0 <= k < CONV_WIDTH:
                    term = cw_ref[k:k + 1, lanes] * abuf_ref[SUBLANES * q:SUBLANES * q + tm + SUBLANES, lanes]
                    p_r = term if p_r is None else p_r + term
            acc = acc + p_r[r:r + tm, :]
        return acc

    conv = jnp.concatenate([conv_lane_block(i) for i in range(d_a // LANES)], axis=1)
    abuf_ref[0:HALO, :] = abuf_ref[tm:tm + HALO, :]
    a_ln = _layernorm(conv, ag_ref[...], ab_ref[...])
    a_out = a_ln * jax.nn.sigmoid(a_ln)

    b_u = proj[:, 2 * d_a:2 * d_a + d_b]
    v = _layernorm(proj[:, 2 * d_a + d_b:], vg_ref[...], vb_ref[...]).astype(BF16)
    row = lax.broadcasted_iota(jnp.int32, (CHUNK, CHUNK), 0)
    col = lax.broadcasted_iota(jnp.int32, (CHUNK, CHUNK), 1)
    causal = row >= col
    mixed = []
    for hd in range(n_heads):
        lanes = slice(hd * HEAD_DIM_B, (hd + 1) * HEAD_DIM_B)
        rhs = jnp.concatenate([v[c * CHUNK:(c + 1) * CHUNK, lanes] for c in range(n_chunks)], axis=1)
        w_c = jnp.where(causal, ws_ref[hd], 0.0).astype(BF16)
        mixed.append(_bdot(w_c, rhs))
    bias = bias_ref[...]
    rows = []
    for c in range(n_chunks):
        lanes = slice(c * HEAD_DIM_B, (c + 1) * HEAD_DIM_B)
        rows.append(jnp.concatenate([mixed[hd][:, lanes] for hd in range(n_heads)], axis=1) + bias)
    b_out = b_u * jnp.concatenate(rows, axis=0)
    cat_ref[...] = jnp.concatenate([a_out, b_out], axis=1).astype(BF16)

    y = jnp.concatenate([_bdot(cat_prev, wout_ref[c]) for c in range(wout_ref.shape[0])], axis=1)
    _store_residual_and_ffn_input(xp_ref[...] + gate * y, m, fg_ref, o_ref, h2_ref)


def _even_layer(x2, mod, norm_g, ffn_g, w_in, conv_w, conv_b, a_g, a_b, v_g, v_b, w_s, bias_full, w_out, l,
                tm=256):
    s, d = x2.shape
    i = l // 2
    n_t = s // tm
    d_a = conv_w.shape[2]
    d_b = v_g.shape[2]
    n_heads = w_s.shape[1]
    vec = lambda width, idx: pl.BlockSpec((None, 1, width), lambda t: (idx, 0, 0))
    cur = lambda t: (jnp.minimum(t, n_t - 1), 0)
    prev = lambda t: (jnp.maximum(t - 1, 0), 0)
    return pl.pallas_call(
        _even_kernel,
        grid=(n_t + 1,),
        in_specs=[
            pl.BlockSpec((tm, d), cur),
            pl.BlockSpec((tm, d), prev),
            pl.BlockSpec((N_MOD, d), lambda t: (0, 0)),
            vec(d, l),
            vec(d, l),
            _resident(w_in.shape, lambda t: (0, 0, 0)),
            pl.BlockSpec((None, CONV_WIDTH, d_a), lambda t: (i, 0, 0)),
            vec(d_a, i), vec(d_a, i), vec(d_a, i), vec(d_b, i), vec(d_b, i),
            pl.BlockSpec((None, n_heads, CHUNK, CHUNK), lambda t: (i, 0, 0, 0)),
            pl.BlockSpec((None, CHUNK, d_b), lambda t: (i, 0, 0)),
            _resident(w_out.shape, lambda t: (0, 0, 0)),
        ],
        out_specs=[pl.BlockSpec((tm, d), prev)] * 2,
        out_shape=[jax.ShapeDtypeStruct((s, d), F32), jax.ShapeDtypeStruct((s, d), BF16)],
        scratch_shapes=[pltpu.VMEM((HALO + tm + SUBLANES, d_a), F32), pltpu.VMEM((tm, d_a + d_b), BF16)],
        compiler_params=_params(("arbitrary",)),
        name="even_mixer",
    )(x2, x2, mod, norm_g, ffn_g, w_in, conv_w, conv_b, a_g, a_b, v_g, v_b, w_s, bias_full, w_out)


def _pool_kernel(x_ref, mod_ref, g_ref, fg_ref, pw_ref, ps_ref, o_ref, h2_ref, hbuf_ref, p_ref, q_ref):
    tm, d = x_ref.shape
    n_groups = len(POOL_WINDOWS)
    dg = d // n_groups
    t0 = pl.program_id(0) * tm

    @pl.when(pl.program_id(0) == 0)
    def _():
        hbuf_ref[0:HALO, :] = jnp.zeros((HALO, d), F32)

    x = x_ref[...]
    m = mod_ref[...]
    shift, scale, gate = m[0:1], m[1:2], m[2:3]
    h = _modulated_rmsnorm(x, g_ref[...], shift, scale)
    hbuf_ref[HALO:HALO + tm, :] = h

    end = HALO + tm
    p_ref[8:end, :] = hbuf_ref[8:end, :] + hbuf_ref[7:end - 1, :]
    q_ref[16:end, dg:] = p_ref[16:end, dg:] + p_ref[14:end - 2, dg:]
    p_ref[24:end, 2 * dg:] = q_ref[24:end, 2 * dg:] + q_ref[20:end - 4, 2 * dg:]
    q_ref[32:end, 3 * dg:] = p_ref[32:end, 3 * dg:] + p_ref[24:end - 8, 3 * dg:]
    sums = (p_ref, q_ref, p_ref, q_ref)

    pos = (lax.broadcasted_iota(jnp.int32, (tm, LANES), 0) + (t0 + 1)).astype(F32)
    ys = []
    for gi, w in enumerate(POOL_WINDOWS):
        lanes = slice(gi * dg, (gi + 1) * dg)
        inv_cnt = jnp.concatenate([1.0 / jnp.minimum(pos, float(w))] * (dg // LANES), axis=1)
        pooled = sums[gi][HALO:end, lanes] * inv_cnt - h[:, lanes]
        ys.append(_bdot(pooled.astype(BF16), pw_ref[gi]))
    y = jnp.concatenate(ys, axis=1) * ps_ref[...]
    _store_residual_and_ffn_input(x + gate * y, m, fg_ref, o_ref, h2_ref)
    hbuf_ref[0:HALO, :] = hbuf_ref[tm:tm + HALO, :]


def _pool_layer(x2, mod, norm_g, ffn_g, pool_w, pool_scale, l, tm=512):
    s, d = x2.shape
    i = l // 2
    n_groups, dg, _ = pool_w.shape[1:]
    return pl.pallas_call(
        _pool_kernel,
        grid=(s // tm,),
        in_specs=[
            pl.BlockSpec((tm, d), lambda t: (t, 0)),
            pl.BlockSpec((N_MOD, d), lambda t: (0, 0)),
            pl.BlockSpec((None, 1, d), lambda t: (l, 0, 0)),
            pl.BlockSpec((None, 1, d), lambda t: (l, 0, 0)),
            _resident((None, n_groups, dg, dg), lambda t: (i, 0, 0, 0)),
            pl.BlockSpec((None, 1, d), lambda t: (i, 0, 0)),
        ],
        out_specs=[pl.BlockSpec((tm, d), lambda t: (t, 0))] * 2,
        out_shape=[jax.ShapeDtypeStruct((s, d), F32), jax.ShapeDtypeStruct((s, d), BF16)],
        scratch_shapes=[pltpu.VMEM((HALO + tm, d), F32)] * 3,
        compiler_params=_params(("arbitrary",)),
        name="pool_mixer",
    )(x2, mod, norm_g, ffn_g, pool_w, pool_scale)


N_FFN_IN = 7


def _ffn_kernel(*refs, final_norm, n_cast, n_x, n_ada):
    h_ref, x_ref, mod_ref, w1_ref, w3_ref, w2_ref, fg_ref = refs[:N_FFN_IN]
    n_in = N_FFN_IN + (3 if n_ada else 0)
    cast_src = refs[n_in:n_in + n_cast]
    o_ref = refs[n_in + n_cast]
    n_out = 2 if n_ada else 1
    cast_dst = refs[n_in + n_cast + n_out:]
    j = pl.program_id(1)
    last = pl.num_programs(1) - 1
    x_rows = x_ref.shape[0]

    @pl.when(j == 0)
    def _():
        o_ref[...] = jnp.zeros(o_ref.shape, F32)

    if n_ada:
        ccol_ref, adaw_ref, adab_ref = refs[N_FFN_IN:n_in]
        modn_ref = refs[n_in + n_cast + 1]

        @pl.when(j == 0)
        def _():
            modn_ref[...] = adab_ref[...]

        ccol = ccol_ref[...]
        part = jnp.sum(ccol * jax.nn.sigmoid(ccol) * adaw_ref[...], axis=0, keepdims=True)
        modn_ref[...] += jnp.where(j < n_ada, part, 0.0)

    r0 = pl.multiple_of(jnp.minimum(j, n_x - 1) * x_rows, x_rows)
    o_ref[pl.ds(r0, x_rows), :] += jnp.where(j < n_x, x_ref[...], 0.0)
    for src, dst in zip(cast_src, cast_dst):
        dst[...] = src[...].astype(BF16)

    h = h_ref[...]
    tf = w1_ref.shape[1]
    us = []
    for lanes in (slice(0, tf // 2), slice(tf // 2, tf)):
        a = _bdot(h, w1_ref[:, lanes])
        b = _bdot(h, w3_ref[:, lanes])
        us.append((a * jax.nn.sigmoid(a) * b).astype(BF16))
    u = jnp.concatenate(us, axis=1)
    o_ref[...] += mod_ref[5:6, :] * _bdot(u, w2_ref[...])

    if final_norm:
        @pl.when(j == last)
        def _():
            out = o_ref[...]
            ms = jnp.mean(out * out, axis=-1, keepdims=True)
            o_ref[...] = out * lax.rsqrt(ms + EPS) * fg_ref[...]


def _cast_job(w, layer, n_i, n_j, column_blocks=False):
    rows, cols = w.shape[1:]
    if column_blocks:
        n_cb, rb = cols // W_BLOCK, rows // n_i
        assert n_cb <= n_j
        in_spec = pl.BlockSpec((None, rb, W_BLOCK), lambda t, j: (layer, t, jnp.minimum(j, n_cb - 1)))
        out_spec = pl.BlockSpec((None, rb, W_BLOCK), lambda t, j: (jnp.minimum(j, n_cb - 1), t, 0))
        return w, in_spec, out_spec, jax.ShapeDtypeStruct((n_cb, rows, W_BLOCK), BF16)
    if rows % (n_i * n_j * BF16_SUBLANES) == 0:
        rb = rows // (n_i * n_j)
        in_spec = pl.BlockSpec((None, rb, cols), lambda t, j: (layer, j * n_i + t, 0))
        out_spec = pl.BlockSpec((rb, cols), lambda t, j: (j * n_i + t, 0))
    else:
        n_cb = max(k for k in range(1, n_j + 1) if cols % (k * LANES) == 0)
        rb, cb = rows // n_i, cols // n_cb
        in_spec = pl.BlockSpec((None, rb, cb), lambda t, j: (layer, t, jnp.minimum(j, n_cb - 1)))
        out_spec = pl.BlockSpec((rb, cb), lambda t, j: (t, jnp.minimum(j, n_cb - 1)))
    return w, in_spec, out_spec, jax.ShapeDtypeStruct((rows, cols), BF16)


def _ffn_layer(h2, x2, mod, w1, w3, w2, final_g, final_norm, next_ada=None, cast_ahead=(), tm=1024, tf=512):
    s, d = x2.shape
    d_ff = w1.shape[1]
    n_i, n_j = s // tm, d_ff // tf
    n_x = max(k for k in range(1, n_j + 1) if tm % (k * SUBLANES) == 0)
    jobs = [_cast_job(w, layer, n_i, n_j, blocks) for w, layer, blocks in cast_ahead]
    ada_in, ada_specs, ada_out_specs, ada_out_shapes, n_ada = [], [], [], [], 0
    if next_ada is not None:
        c_cols, ada_w, ada_b3, layer = next_ada
        n_ada, kb, _ = c_cols.shape
        n_mod = ada_w.shape[2]
        cb = n_mod // n_i
        assert n_ada <= n_j and cb % LANES == 0
        kblk = lambda j: jnp.minimum(j, n_ada - 1)
        ada_in = [c_cols, ada_w, ada_b3]
        ada_specs = [
            pl.BlockSpec((None, kb, 1), lambda t, j: (kblk(j), 0, 0)),
            pl.BlockSpec((None, kb, cb), lambda t, j: (layer, kblk(j), t)),
            pl.BlockSpec((None, 1, cb), lambda t, j: (layer, 0, t)),
        ]
        ada_out_specs = [pl.BlockSpec((1, cb), lambda t, j: (0, t))]
        ada_out_shapes = [jax.ShapeDtypeStruct((1, n_mod), F32)]
    outs = pl.pallas_call(
        functools.partial(_ffn_kernel, final_norm=final_norm, n_cast=len(jobs), n_x=n_x, n_ada=n_ada),
        grid=(n_i, n_j),
        in_specs=[
            pl.BlockSpec((tm, d), lambda t, j: (t, 0)),
            pl.BlockSpec((tm // n_x, d), lambda t, j: (t * n_x + jnp.minimum(j, n_x - 1), 0)),
            pl.BlockSpec((N_MOD, d), lambda t, j: (0, 0)),
            pl.BlockSpec((d, tf), lambda t, j: (0, j)),
            pl.BlockSpec((d, tf), lambda t, j: (0, j)),
            pl.BlockSpec((tf, d), lambda t, j: (j, 0)),
            pl.BlockSpec((1, d), lambda t, j: (0, 0)),
        ] + ada_specs + [job[1] for job in jobs],
        out_specs=[pl.BlockSpec((tm, d), lambda t, j: (t, 0))] + ada_out_specs + [job[2] for job in jobs],
        out_shape=[jax.ShapeDtypeStruct((s, d), F32)] + ada_out_shapes + [job[3] for job in jobs],
        compiler_params=_params(("arbitrary", "arbitrary")),
        name="ffn",
    )(h2, x2, mod, w1, w3, w2, final_g, *ada_in, *[job[0] for job in jobs])
    n_fixed = 1 + len(ada_out_shapes)
    mod_next = outs[1].reshape(N_MOD, d) if next_ada is not None else None
    return outs[0], mod_next, outs[n_fixed:]


def kernel(x, c, ada_w, ada_b, norm_mix_g, norm_ffn_g, ab_w_in, a_conv_w, a_conv_b, a_norm_g, a_norm_b,
           b_norm_g, b_norm_b, b_w_s, b_bias, ab_w_out, pool_w, pool_scale, ffn_w1, ffn_w3, ffn_w2, final_g):
    bsz, s, d = x.shape
    depth = ada_w.shape[0]
    assert bsz == 1, "per-sequence adaLN terms are passed as single rows"
    row3 = lambda p: p.reshape(p.shape[0], 1, p.shape[1])

    ada_b3 = row3(ada_b)
    mod = _adaln_first(c, ada_w, ada_b3)
    c_cols = c.reshape(ADA_ROW_BLOCKS, d // ADA_ROW_BLOCKS, 1)
    n_even, n_heads, chunk = b_bias.shape
    bias_full = jnp.broadcast_to(
        jnp.transpose(b_bias, (0, 2, 1))[:, :, :, None], (n_even, chunk, n_heads, HEAD_DIM_B)
    ).reshape(n_even, chunk, n_heads * HEAD_DIM_B)

    p_w = pool_w.astype(BF16)
    col_blocks = lambda w: w.astype(BF16).reshape(w.shape[0], -1, W_BLOCK).transpose(1, 0, 2)
    w_in, w_out = col_blocks(ab_w_in[0]), col_blocks(ab_w_out[0])
    w1, w3, w2 = ffn_w1[0].astype(BF16), ffn_w3[0].astype(BF16), ffn_w2[0].astype(BF16)
    mix_g, ffn_g = row3(norm_mix_g), row3(norm_ffn_g)
    conv_b, a_g, a_b, v_g, v_b = (row3(p) for p in (a_conv_b, a_norm_g, a_norm_b, b_norm_g, b_norm_b))
    p_scale = row3(pool_scale)
    fg = final_g.reshape(1, d)

    x2 = x.reshape(s, d)
    for l in range(depth):
        if l % 2 == 0:
            x2, h2 = _even_layer(x2, mod, mix_g, ffn_g, w_in, a_conv_w, conv_b, a_g, a_b, v_g, v_b, b_w_s,
                                 bias_full, w_out, l)
        else:
            x2, h2 = _pool_layer(x2, mod, mix_g, ffn_g, p_w, p_scale, l)
        cast_ahead, next_ada = [], None
        if l + 1 < depth:
            next_ada = (c_cols, ada_w, ada_b3, l + 1)
            cast_ahead += [(ffn_w1, l + 1, False), (ffn_w3, l + 1, False), (ffn_w2, l + 1, False)]
            if (l + 1) % 2 == 0:
                cast_ahead += [(ab_w_in, (l + 1) // 2, True), (ab_w_out, (l + 1) // 2, True)]
        x2, mod, casts = _ffn_layer(h2, x2, mod, w1, w3, w2, fg, final_norm=(l == depth - 1),
                                    next_ada=next_ada, cast_ahead=cast_ahead)
        if casts:
            w1, w3, w2 = casts[:3]
            if len(casts) > 3:
                w_in, w_out = casts[3:]
    return x2.reshape(bsz, s, d)
```

```python
import functools

import jax
import jax.numpy as jnp
from jax import lax
from jax.experimental import pallas as pl
from jax.experimental.pallas import tpu as pltpu

F32 = jnp.float32
BF16 = jnp.bfloat16

EPS = 1e-6
CONV_WIDTH = 31
CHUNK = 128
HEAD_DIM_B = 128
POOL_WINDOWS = (2, 4, 8, 16)
N_MOD = 6

HALO = 32
SUBLANES = 8
BF16_SUBLANES = 16
LANES = 128
W_BLOCK = 512
ADA_ROW_BLOCKS = 8
VMEM_LIMIT_V7X = 56 * 1024 * 1024


def _params(semantics):
    return pltpu.CompilerParams(dimension_semantics=semantics, vmem_limit_bytes=VMEM_LIMIT_V7X)


def _resident(shape, index_map):
    return pl.BlockSpec(shape, index_map, pipeline_mode=pl.Buffered(1))


def _bdot(a, b):
    return jnp.dot(a, b, preferred_element_type=F32)


def _modulated_rmsnorm(x, gain, shift, scale):
    ms = jnp.mean(x * x, axis=-1, keepdims=True)
    return x * lax.rsqrt(ms + EPS) * (gain * (1.0 + scale)) + shift


def _layernorm(x, g, b):
    mu = jnp.mean(x, axis=-1, keepdims=True)
    xc = x - mu
    var = jnp.mean(xc * xc, axis=-1, keepdims=True)
    return xc * lax.rsqrt(var + EPS) * g + b


def _store_residual_and_ffn_input(x_new, m, fg_ref, o_ref, h2_ref):
    o_ref[...] = x_new
    h2_ref[...] = _modulated_rmsnorm(x_new, fg_ref[...], m[3:4], m[4:5]).astype(BF16)


def _row_cast_job(w, layer, n_steps, column_blocks=False):
    rows, cols = w.shape[1:]
    rb = rows // n_steps
    assert rb * n_steps == rows and rb % BF16_SUBLANES == 0
    in_spec = pl.BlockSpec((None, rb, cols), lambda t: (layer, t, 0))
    if column_blocks:
        n_cb = cols // W_BLOCK
        return (w, in_spec, pl.BlockSpec((n_cb, rb, W_BLOCK), lambda t: (0, t, 0)),
                jax.ShapeDtypeStruct((n_cb, rows, W_BLOCK), BF16))
    return w, in_spec, pl.BlockSpec((rb, cols), lambda t: (t, 0)), jax.ShapeDtypeStruct((rows, cols), BF16)


def _run_casts(cast_src, cast_dst):
    for src, dst in zip(cast_src, cast_dst):
        if len(dst.shape) == 3:
            for c in range(dst.shape[0]):
                dst[c] = src[:, c * W_BLOCK:(c + 1) * W_BLOCK].astype(BF16)
        else:
            dst[...] = src[...].astype(BF16)


def _adaln_kernel(*refs, n_cast):
    c_ref, w_ref, b_ref = refs[:3]
    o_ref = refs[3 + n_cast]
    c = c_ref[...]
    cond = c * jax.nn.sigmoid(c)
    cond8 = jnp.broadcast_to(cond, (8, cond.shape[1])).astype(BF16)
    y = _bdot(cond8, w_ref[...].astype(BF16))
    o_ref[...] = y[0:1, :] + b_ref[...]
    _run_casts(refs[3:3 + n_cast], refs[4 + n_cast:])


def _adaln_first(c, ada_w, ada_b3, cast_ahead=(), tn=1536):
    _, d, n = ada_w.shape
    jobs = [_row_cast_job(w, layer, n // tn, blocks) for w, layer, blocks in cast_ahead]
    outs = pl.pallas_call(
        functools.partial(_adaln_kernel, n_cast=len(jobs)),
        grid=(n // tn,),
        in_specs=[
            pl.BlockSpec((1, d), lambda j: (0, 0)),
            pl.BlockSpec((None, d, tn), lambda j: (0, 0, j)),
            pl.BlockSpec((None, 1, tn), lambda j: (0, 0, j)),
        ] + [job[1] for job in jobs],
        out_specs=[pl.BlockSpec((1, tn), lambda j: (0, j))] + [job[2] for job in jobs],
        out_shape=[jax.ShapeDtypeStruct((1, n), F32)] + [job[3] for job in jobs],
        compiler_params=_params(("arbitrary",)),
        name="adaln",
    )(c, ada_w, ada_b3, *[job[0] for job in jobs])
    return outs[0].reshape(N_MOD, d), outs[1:]


N_EVEN_IN = 12


def _even_kernel(*refs, n_cast):
    (x_ref, mod_ref, g_ref, win_ref, cw_ref, cb_ref, ag_ref, ab_ref, vg_ref, vb_ref,
     ws_ref, bias_ref) = refs[:N_EVEN_IN]
    cat_ref, abuf_ref = refs[N_EVEN_IN + n_cast], refs[-1]
    _run_casts(refs[N_EVEN_IN:N_EVEN_IN + n_cast], refs[N_EVEN_IN + n_cast + 1:-1])
    tm = x_ref.shape[0]
    d_a = cw_ref.shape[1]
    n_heads = ws_ref.shape[0]
    n_chunks = tm // CHUNK
    d_b = n_heads * HEAD_DIM_B

    @pl.when(pl.program_id(0) == 0)
    def _():
        abuf_ref[0:HALO, :] = jnp.zeros((HALO, d_a), F32)
        abuf_ref[HALO + tm:HALO + tm + SUBLANES, :] = jnp.zeros((SUBLANES, d_a), F32)

    m = mod_ref[...]
    shift, scale = m[0:1], m[1:2]

    h = _modulated_rmsnorm(x_ref[...], g_ref[...], shift, scale).astype(BF16)
    proj = jnp.concatenate([_bdot(h, win_ref[c]) for c in range(win_ref.shape[0])], axis=1)

    abuf_ref[HALO:HALO + tm, :] = proj[:, 0:d_a] * jax.nn.sigmoid(proj[:, d_a:2 * d_a])
    first = HALO - (CONV_WIDTH - 1)

    def conv_lane_block(i):
        lanes = slice(i * LANES, (i + 1) * LANES)
        acc = cb_ref[:, lanes]
        for r in range(SUBLANES):
            p_r = None
            for q in range((first + CONV_WIDTH - 1) // SUBLANES + 1):
                k = SUBLANES * q + r - first
                if 0 <= k < CONV_WIDTH:
                    term = cw_ref[k:k + 1, lanes] * abuf_ref[SUBLANES * q:SUBLANES * q + tm + SUBLANES, lanes]
                    p_r = term if p_r is None else p_r + term
            acc = acc + p_r[r:r + tm, :]
        return acc

    conv = jnp.concatenate([conv_lane_block(i) for i in range(d_a // LANES)], axis=1)
    abuf_ref[0:HALO, :] = abuf_ref[tm:tm + HALO, :]
    a_ln = _layernorm(conv, ag_ref[...], ab_ref[...])
    a_out = a_ln * jax.nn.sigmoid(a_ln)

    b_u = proj[:, 2 * d_a:2 * d_a + d_b]
    v = _layernorm(proj[:, 2 * d_a + d_b:], vg_ref[...], vb_ref[...]).astype(BF16)
    row = lax.broadcasted_iota(jnp.int32, (CHUNK, CHUNK), 0)
    col = lax.broadcasted_iota(jnp.int32, (CHUNK, CHUNK), 1)
    causal = row >= col
    mixed = []
    for hd in range(n_heads):
        lanes = slice(hd * HEAD_DIM_B, (hd + 1) * HEAD_DIM_B)
        rhs = jnp.concatenate([v[c * CHUNK:(c + 1) * CHUNK, lanes] for c in range(n_chunks)], axis=1)
        w_c = jnp.where(causal, ws_ref[hd], 0.0).astype(BF16)
        mixed.append(_bdot(w_c, rhs))
    bias = bias_ref[...]
    rows = []
    for c in range(n_chunks):
        lanes = slice(c * HEAD_DIM_B, (c + 1) * HEAD_DIM_B)
        rows.append(jnp.concatenate([mixed[hd][:, lanes] for hd in range(n_heads)], axis=1) + bias)
    b_out = b_u * jnp.concatenate(rows, axis=0)
    cat_ref[...] = jnp.concatenate([a_out, b_out], axis=1).astype(BF16)


N_OUT_PROJ_IN = 5


def _out_proj_kernel(*refs, sub, n_cast):
    cat_ref, x_ref, mod_ref, fg_ref, wout_ref = refs[:N_OUT_PROJ_IN]
    o_ref, h2_ref = refs[N_OUT_PROJ_IN + n_cast:N_OUT_PROJ_IN + n_cast + 2]
    _run_casts(refs[N_OUT_PROJ_IN:N_OUT_PROJ_IN + n_cast], refs[N_OUT_PROJ_IN + n_cast + 2:])
    m = mod_ref[...]
    gate = m[2:3]
    for r in range(0, cat_ref.shape[0], sub):
        rows = slice(r, r + sub)
        cat = cat_ref[rows, :]
        y = jnp.concatenate([_bdot(cat, wout_ref[c]) for c in range(wout_ref.shape[0])], axis=1)
        x_new = x_ref[rows, :] + gate * y
        o_ref[rows, :] = x_new
        h2_ref[rows, :] = _modulated_rmsnorm(x_new, fg_ref[...], m[3:4], m[4:5]).astype(BF16)


def _even_layer(x2, mod, norm_g, ffn_g, w_in, conv_w, conv_b, a_g, a_b, v_g, v_b, w_s, bias_full, w_out, l,
                mixer_casts=(), out_proj_casts=(), tm=256, tm_out=512, sub_out=256):
    s, d = x2.shape
    i = l // 2
    d_a = conv_w.shape[2]
    d_b = v_g.shape[2]
    n_heads = w_s.shape[1]
    vec = lambda width, idx: pl.BlockSpec((None, 1, width), lambda t: (idx, 0, 0))
    jobs = [_row_cast_job(w, layer, s // tm, blocks) for w, layer, blocks in mixer_casts]
    outs = pl.pallas_call(
        functools.partial(_even_kernel, n_cast=len(jobs)),
        grid=(s // tm,),
        in_specs=[
            pl.BlockSpec((tm, d), lambda t: (t, 0)),
            pl.BlockSpec((N_MOD, d), lambda t: (0, 0)),
            vec(d, l),
            _resident(w_in.shape, lambda t: (0, 0, 0)),
            pl.BlockSpec((None, CONV_WIDTH, d_a), lambda t: (i, 0, 0)),
            vec(d_a, i), vec(d_a, i), vec(d_a, i), vec(d_b, i), vec(d_b, i),
            pl.BlockSpec((None, n_heads, CHUNK, CHUNK), lambda t: (i, 0, 0, 0)),
            pl.BlockSpec((None, CHUNK, d_b), lambda t: (i, 0, 0)),
        ] + [job[1] for job in jobs],
        out_specs=[pl.BlockSpec((tm, d_a + d_b), lambda t: (t, 0))] + [job[2] for job in jobs],
        out_shape=[jax.ShapeDtypeStruct((s, d_a + d_b), BF16)] + [job[3] for job in jobs],
        scratch_shapes=[pltpu.VMEM((HALO + tm + SUBLANES, d_a), F32)],
        compiler_params=_params(("arbitrary",)),
        name="even_mixer",
    )(x2, mod, norm_g, w_in, conv_w, conv_b, a_g, a_b, v_g, v_b, w_s, bias_full, *[job[0] for job in jobs])
    cat, casts_mixer = outs[0], outs[1:]
    jobs = [_row_cast_job(w, layer, s // tm_out, blocks) for w, layer, blocks in out_proj_casts]
    outs = pl.pallas_call(
        functools.partial(_out_proj_kernel, sub=sub_out, n_cast=len(jobs)),
        grid=(s // tm_out,),
        in_specs=[
            pl.BlockSpec((tm_out, d_a + d_b), lambda t: (t, 0)),
            pl.BlockSpec((tm_out, d), lambda t: (t, 0)),
            pl.BlockSpec((N_MOD, d), lambda t: (0, 0)),
            vec(d, l),
            _resident(w_out.shape, lambda t: (0, 0, 0)),
        ] + [job[1] for job in jobs],
        out_specs=[pl.BlockSpec((tm_out, d), lambda t: (t, 0))] * 2 + [job[2] for job in jobs],
        out_shape=[jax.ShapeDtypeStruct((s, d), F32), jax.ShapeDtypeStruct((s, d), BF16)] + [job[3] for job in jobs],
        compiler_params=_params(("arbitrary",)),
        name="even_out_proj",
    )(cat, x2, mod, ffn_g, w_out, *[job[0] for job in jobs])
    return outs[0], outs[1], casts_mixer, outs[2:]


def _pool_kernel(x_ref, mod_ref, g_ref, fg_ref, pw_ref, ps_ref, o_ref, h2_ref, hbuf_ref, p_ref, q_ref):
    tm, d = x_ref.shape
    n_groups = len(POOL_WINDOWS)
    dg = d // n_groups
    t0 = pl.program_id(0) * tm

    @pl.when(pl.program_id(0) == 0)
    def _():
        hbuf_ref[0:HALO, :] = jnp.zeros((HALO, d), F32)

    x = x_ref[...]
    m = mod_ref[...]
    shift, scale, gate = m[0:1], m[1:2], m[2:3]
    h = _modulated_rmsnorm(x, g_ref[...], shift, scale)
    hbuf_ref[HALO:HALO + tm, :] = h

    end = HALO + tm
    p_ref[8:end, :] = hbuf_ref[8:end, :] + hbuf_ref[7:end - 1, :]
    q_ref[16:end, dg:] = p_ref[16:end, dg:] + p_ref[14:end - 2, dg:]
    p_ref[24:end, 2 * dg:] = q_ref[24:end, 2 * dg:] + q_ref[20:end - 4, 2 * dg:]
    q_ref[32:end, 3 * dg:] = p_ref[32:end, 3 * dg:] + p_ref[24:end - 8, 3 * dg:]
    sums = (p_ref, q_ref, p_ref, q_ref)

    pos = (lax.broadcasted_iota(jnp.int32, (tm, LANES), 0) + (t0 + 1)).astype(F32)
    ys = []
    for gi, w in enumerate(POOL_WINDOWS):
        lanes = slice(gi * dg, (gi + 1) * dg)
        inv_cnt = jnp.concatenate([1.0 / jnp.minimum(pos, float(w))] * (dg // LANES), axis=1)
        pooled = sums[gi][HALO:end, lanes] * inv_cnt - h[:, lanes]
        ys.append(_bdot(pooled.astype(BF16), pw_ref[gi]))
    y = jnp.concatenate(ys, axis=1) * ps_ref[...]
    _store_residual_and_ffn_input(x + gate * y, m, fg_ref, o_ref, h2_ref)
    hbuf_ref[0:HALO, :] = hbuf_ref[tm:tm + HALO, :]


def _pool_layer(x2, mod, norm_g, ffn_g, pool_w, pool_scale, l, tm=512):
    s, d = x2.shape
    i = l // 2
    n_groups, dg, _ = pool_w.shape[1:]
    return pl.pallas_call(
        _pool_kernel,
        grid=(s // tm,),
        in_specs=[
            pl.BlockSpec((tm, d), lambda t: (t, 0)),
            pl.BlockSpec((N_MOD, d), lambda t: (0, 0)),
            pl.BlockSpec((None, 1, d), lambda t: (l, 0, 0)),
            pl.BlockSpec((None, 1, d), lambda t: (l, 0, 0)),
            _resident((None, n_groups, dg, dg), lambda t: (i, 0, 0, 0)),
            pl.BlockSpec((None, 1, d), lambda t: (i, 0, 0)),
        ],
        out_specs=[pl.BlockSpec((tm, d), lambda t: (t, 0))] * 2,
        out_shape=[jax.ShapeDtypeStruct((s, d), F32), jax.ShapeDtypeStruct((s, d), BF16)],
        scratch_shapes=[pltpu.VMEM((HALO + tm, d), F32)] * 3,
        compiler_params=_params(("arbitrary",)),
        name="pool_mixer",
    )(x2, mod, norm_g, ffn_g, pool_w, pool_scale)


N_FFN_IN = 7


def _ffn_kernel(*refs, final_norm, n_cast, n_x, n_ada):
    h_ref, x_ref, mod_ref, w1_ref, w3_ref, w2_ref, fg_ref = refs[:N_FFN_IN]
    n_in = N_FFN_IN + (3 if n_ada else 0)
    cast_src = refs[n_in:n_in + n_cast]
    o_ref = refs[n_in + n_cast]
    n_out = 2 if n_ada else 1
    cast_dst = refs[n_in + n_cast + n_out:]
    j = pl.program_id(1)
    last = pl.num_programs(1) - 1
    x_rows = x_ref.shape[0]

    @pl.when(j == 0)
    def _():
        o_ref[...] = jnp.zeros(o_ref.shape, F32)

    if n_ada:
        ccol_ref, adaw_ref, adab_ref = refs[N_FFN_IN:n_in]
        modn_ref = refs[n_in + n_cast + 1]

        @pl.when(j == 0)
        def _():
            modn_ref[...] = adab_ref[...]

        ccol = ccol_ref[...]
        part = jnp.sum(ccol * jax.nn.sigmoid(ccol) * adaw_ref[...], axis=0, keepdims=True)
        modn_ref[...] += jnp.where(j < n_ada, part, 0.0)

    r0 = pl.multiple_of(jnp.minimum(j, n_x - 1) * x_rows, x_rows)
    o_ref[pl.ds(r0, x_rows), :] += jnp.where(j < n_x, x_ref[...], 0.0)
    for src, dst in zip(cast_src, cast_dst):
        dst[...] = src[...].astype(BF16)

    h = h_ref[...]
    tf = w1_ref.shape[1]
    us = []
    for lanes in (slice(0, tf // 2), slice(tf // 2, tf)):
        a = _bdot(h, w1_ref[:, lanes])
        b = _bdot(h, w3_ref[:, lanes])
        us.append((a * jax.nn.sigmoid(a) * b).astype(BF16))
    u = jnp.concatenate(us, axis=1)
    o_ref[...] += mod_ref[5:6, :] * _bdot(u, w2_ref[...])

    if final_norm:
        @pl.when(j == last)
        def _():
            out = o_ref[...]
            ms = jnp.mean(out * out, axis=-1, keepdims=True)
            o_ref[...] = out * lax.rsqrt(ms + EPS) * fg_ref[...]


def _cast_job(w, layer, n_i, n_j, column_blocks=False):
    rows, cols = w.shape[1:]
    if column_blocks:
        n_cb, rb = cols // W_BLOCK, rows // n_i
        assert n_cb <= n_j
        in_spec = pl.BlockSpec((None, rb, W_BLOCK), lambda t, j: (layer, t, jnp.minimum(j, n_cb - 1)))
        out_spec = pl.BlockSpec((None, rb, W_BLOCK), lambda t, j: (jnp.minimum(j, n_cb - 1), t, 0))
        return w, in_spec, out_spec, jax.ShapeDtypeStruct((n_cb, rows, W_BLOCK), BF16)
    if rows % (n_i * n_j * BF16_SUBLANES) == 0:
        rb = rows // (n_i * n_j)
        in_spec = pl.BlockSpec((None, rb, cols), lambda t, j: (layer, j * n_i + t, 0))
        out_spec = pl.BlockSpec((rb, cols), lambda t, j: (j * n_i + t, 0))
    else:
        n_cb = max(k for k in range(1, n_j + 1) if cols % (k * LANES) == 0)
        rb, cb = rows // n_i, cols // n_cb
        in_spec = pl.BlockSpec((None, rb, cb), lambda t, j: (layer, t, jnp.minimum(j, n_cb - 1)))
        out_spec = pl.BlockSpec((rb, cb), lambda t, j: (t, jnp.minimum(j, n_cb - 1)))
    return w, in_spec, out_spec, jax.ShapeDtypeStruct((rows, cols), BF16)


def _ffn_layer(h2, x2, mod, w1, w3, w2, final_g, final_norm, next_ada=None, cast_ahead=(), tm=1024, tf=512):
    s, d = x2.shape
    d_ff = w1.shape[1]
    n_i, n_j = s // tm, d_ff // tf
    n_x = max(k for k in range(1, n_j + 1) if tm % (k * SUBLANES) == 0)
    jobs = [_cast_job(w, layer, n_i, n_j, blocks) for w, layer, blocks in cast_ahead]
    ada_in, ada_specs, ada_out_specs, ada_out_shapes, n_ada = [], [], [], [], 0
    if next_ada is not None:
        c_cols, ada_w, ada_b3, layer = next_ada
        n_ada, kb, _ = c_cols.shape
        n_mod = ada_w.shape[2]
        cb = n_mod // n_i
        assert n_ada <= n_j and cb % LANES == 0
        kblk = lambda j: jnp.minimum(j, n_ada - 1)
        ada_in = [c_cols, ada_w, ada_b3]
        ada_specs = [
            pl.BlockSpec((None, kb, 1), lambda t, j: (kblk(j), 0, 0)),
            pl.BlockSpec((None, kb, cb), lambda t, j: (layer, kblk(j), t)),
            pl.BlockSpec((None, 1, cb), lambda t, j: (layer, 0, t)),
        ]
        ada_out_specs = [pl.BlockSpec((1, cb), lambda t, j: (0, t))]
        ada_out_shapes = [jax.ShapeDtypeStruct((1, n_mod), F32)]
    outs = pl.pallas_call(
        functools.partial(_ffn_kernel, final_norm=final_norm, n_cast=len(jobs), n_x=n_x, n_ada=n_ada),
        grid=(n_i, n_j),
        in_specs=[
            pl.BlockSpec((tm, d), lambda t, j: (t, 0)),
            pl.BlockSpec((tm // n_x, d), lambda t, j: (t * n_x + jnp.minimum(j, n_x - 1), 0)),
            pl.BlockSpec((N_MOD, d), lambda t, j: (0, 0)),
            pl.BlockSpec((d, tf), lambda t, j: (0, j)),
            pl.BlockSpec((d, tf), lambda t, j: (0, j)),
            pl.BlockSpec((tf, d), lambda t, j: (j, 0)),
            pl.BlockSpec((1, d), lambda t, j: (0, 0)),
        ] + ada_specs + [job[1] for job in jobs],
        out_specs=[pl.BlockSpec((tm, d), lambda t, j: (t, 0))] + ada_out_specs + [job[2] for job in jobs],
        out_shape=[jax.ShapeDtypeStruct((s, d), F32)] + ada_out_shapes + [job[3] for job in jobs],
        compiler_params=_params(("arbitrary", "arbitrary")),
        name="ffn",
    )(h2, x2, mod, w1, w3, w2, final_g, *ada_in, *[job[0] for job in jobs])
    n_fixed = 1 + len(ada_out_shapes)
    mod_next = outs[1].reshape(N_MOD, d) if next_ada is not None else None
    return outs[0], mod_next, outs[n_fixed:]


def kernel(x, c, ada_w, ada_b, norm_mix_g, norm_ffn_g, ab_w_in, a_conv_w, a_conv_b, a_norm_g, a_norm_b,
           b_norm_g, b_norm_b, b_w_s, b_bias, ab_w_out, pool_w, pool_scale, ffn_w1, ffn_w3, ffn_w2, final_g):
    bsz, s, d = x.shape
    depth = ada_w.shape[0]
    assert bsz == 1, "per-sequence adaLN terms are passed as single rows"
    row3 = lambda p: p.reshape(p.shape[0], 1, p.shape[1])

    ada_b3 = row3(ada_b)
    mod, (w_in, w_out) = _adaln_first(c, ada_w, ada_b3, [(ab_w_in, 0, True), (ab_w_out, 0, True)])
    c_cols = c.reshape(ADA_ROW_BLOCKS, d // ADA_ROW_BLOCKS, 1)
    n_even, n_heads, chunk = b_bias.shape
    bias_full = jnp.broadcast_to(
        jnp.transpose(b_bias, (0, 2, 1))[:, :, :, None], (n_even, chunk, n_heads, HEAD_DIM_B)
    ).reshape(n_even, chunk, n_heads * HEAD_DIM_B)

    p_w = pool_w.astype(BF16)
    w1 = w3 = w2 = None
    mix_g, ffn_g = row3(norm_mix_g), row3(norm_ffn_g)
    conv_b, a_g, a_b, v_g, v_b = (row3(p) for p in (a_conv_b, a_norm_g, a_norm_b, b_norm_g, b_norm_b))
    p_scale = row3(pool_scale)
    fg = final_g.reshape(1, d)

    x2 = x.reshape(s, d)
    for l in range(depth):
        if l % 2 == 0:
            first = l == 0
            x2, h2, casts_m, casts_o = _even_layer(
                x2, mod, mix_g, ffn_g, w_in, a_conv_w, conv_b, a_g, a_b, v_g, v_b, b_w_s, bias_full, w_out, l,
                mixer_casts=[(ffn_w2, 0, False)] if first else (),
                out_proj_casts=[(ffn_w1, 0, False), (ffn_w3, 0, False)] if first else ())
            if first:
                (w2,), (w1, w3) = casts_m, casts_o
        else:
            x2, h2 = _pool_layer(x2, mod, mix_g, ffn_g, p_w, p_scale, l)
        cast_ahead, next_ada = [], None
        if l + 1 < depth:
            next_ada = (c_cols, ada_w, ada_b3, l + 1)
            cast_ahead += [(ffn_w1, l + 1, False), (ffn_w3, l + 1, False), (ffn_w2, l + 1, False)]
            if (l + 1) % 2 == 0:
                cast_ahead += [(ab_w_in, (l + 1) // 2, True), (ab_w_out, (l + 1) // 2, True)]
        x2, mod, casts = _ffn_layer(h2, x2, mod, w1, w3, w2, fg, final_norm=(l == depth - 1),
                                    next_ada=next_ada, cast_ahead=cast_ahead)
        if casts:
            w1, w3, w2 = casts[:3]
            if len(casts) > 3:
                w_in, w_out = casts[3:]
    return x2.reshape(bsz, s, d)
```

```python
import functools

import jax
import jax.numpy as jnp
from jax import lax
from jax.experimental import pallas as pl
from jax.experimental.pallas import tpu as pltpu

F32 = jnp.float32
BF16 = jnp.bfloat16

EPS = 1e-6
CONV_WIDTH = 31
CHUNK = 128
HEAD_DIM_B = 128
POOL_WINDOWS = (2, 4, 8, 16)
N_MOD = 6

HALO = 32
SUBLANES = 8
BF16_SUBLANES = 16
LANES = 128
W_BLOCK = 512
ADA_ROW_BLOCKS = 8
VMEM_LIMIT_V7X = 56 * 1024 * 1024


def _params(semantics):
    return pltpu.CompilerParams(dimension_semantics=semantics, vmem_limit_bytes=VMEM_LIMIT_V7X)


def _resident(shape, index_map):
    return pl.BlockSpec(shape, index_map, pipeline_mode=pl.Buffered(1))


def _bdot(a, b):
    return jnp.dot(a, b, preferred_element_type=F32)


def _modulated_rmsnorm(x, gain, shift, scale):
    ms = jnp.mean(x * x, axis=-1, keepdims=True)
    return x * lax.rsqrt(ms + EPS) * (gain * (1.0 + scale)) + shift


def _layernorm(x, g, b):
    mu = jnp.mean(x, axis=-1, keepdims=True)
    xc = x - mu
    var = jnp.mean(xc * xc, axis=-1, keepdims=True)
    return xc * lax.rsqrt(var + EPS) * g + b


def _store_residual_and_ffn_input(x_new, m, fg_ref, o_ref, h2_ref):
    o_ref[...] = x_new
    h2_ref[...] = _modulated_rmsnorm(x_new, fg_ref[...], m[3:4], m[4:5]).astype(BF16)


def _row_cast_job(w, layer, n_steps, column_blocks=False):
    rows, cols = w.shape[1:]
    rb = rows // n_steps
    assert rb * n_steps == rows and rb % BF16_SUBLANES == 0
    in_spec = pl.BlockSpec((None, rb, cols), lambda t: (layer, t, 0))
    if column_blocks:
        n_cb = cols // W_BLOCK
        return (w, in_spec, pl.BlockSpec((n_cb, rb, W_BLOCK), lambda t: (0, t, 0)),
                jax.ShapeDtypeStruct((n_cb, rows, W_BLOCK), BF16))
    return w, in_spec, pl.BlockSpec((rb, cols), lambda t: (t, 0)), jax.ShapeDtypeStruct((rows, cols), BF16)


def _run_casts(cast_src, cast_dst):
    for src, dst in zip(cast_src, cast_dst):
        if len(dst.shape) == 3:
            for c in range(dst.shape[0]):
                dst[c] = src[:, c * W_BLOCK:(c + 1) * W_BLOCK].astype(BF16)
        else:
            dst[...] = src[...].astype(BF16)


def _adaln_kernel(*refs, n_cast):
    c_ref, w_ref, b_ref = refs[:3]
    o_ref = refs[3 + n_cast]
    c = c_ref[...]
    cond = c * jax.nn.sigmoid(c)
    cond8 = jnp.broadcast_to(cond, (8, cond.shape[1])).astype(BF16)
    y = _bdot(cond8, w_ref[...].astype(BF16))
    o_ref[...] = y[0:1, :] + b_ref[...]
    _run_casts(refs[3:3 + n_cast], refs[4 + n_cast:])


def _adaln_first(c, ada_w, ada_b3, cast_ahead=(), tn=1536):
    _, d, n = ada_w.shape
    jobs = [_row_cast_job(w, layer, n // tn, blocks) for w, layer, blocks in cast_ahead]
    outs = pl.pallas_call(
        functools.partial(_adaln_kernel, n_cast=len(jobs)),
        grid=(n // tn,),
        in_specs=[
            pl.BlockSpec((1, d), lambda j: (0, 0)),
            pl.BlockSpec((None, d, tn), lambda j: (0, 0, j)),
            pl.BlockSpec((None, 1, tn), lambda j: (0, 0, j)),
        ] + [job[1] for job in jobs],
        out_specs=[pl.BlockSpec((1, tn), lambda j: (0, j))] + [job[2] for job in jobs],
        out_shape=[jax.ShapeDtypeStruct((1, n), F32)] + [job[3] for job in jobs],
        compiler_params=_params(("arbitrary",)),
        name="adaln",
    )(c, ada_w, ada_b3, *[job[0] for job in jobs])
    return outs[0].reshape(N_MOD, d), outs[1:]


N_EVEN_IN = 12


def _even_kernel(*refs, n_cast, sub):
    (x_ref, mod_ref, g_ref, win_ref, cw_ref, cb_ref, ag_ref, ab_ref, vg_ref, vb_ref,
     ws_ref, bias_ref) = refs[:N_EVEN_IN]
    cat_ref, abuf_ref = refs[N_EVEN_IN + n_cast], refs[-1]
    _run_casts(refs[N_EVEN_IN:N_EVEN_IN + n_cast], refs[N_EVEN_IN + n_cast + 1:-1])
    d_a = cw_ref.shape[1]
    n_heads = ws_ref.shape[0]
    n_chunks = sub // CHUNK
    d_b = n_heads * HEAD_DIM_B

    @pl.when(pl.program_id(0) == 0)
    def _():
        abuf_ref[0:HALO, :] = jnp.zeros((HALO, d_a), F32)
        abuf_ref[HALO + sub:HALO + sub + SUBLANES, :] = jnp.zeros((SUBLANES, d_a), F32)

    m = mod_ref[...]
    shift, scale = m[0:1], m[1:2]

    def project(r):
        h = _modulated_rmsnorm(x_ref[r * sub:(r + 1) * sub, :], g_ref[...], shift, scale).astype(BF16)
        return jnp.concatenate([_bdot(h, win_ref[c]) for c in range(win_ref.shape[0])], axis=1)

    first = HALO - (CONV_WIDTH - 1)

    def conv_lane_block(i):
        lanes = slice(i * LANES, (i + 1) * LANES)
        acc = cb_ref[:, lanes]
        for r in range(SUBLANES):
            p_r = None
            for q in range((first + CONV_WIDTH - 1) // SUBLANES + 1):
                k = SUBLANES * q + r - first
                if 0 <= k < CONV_WIDTH:
                    term = cw_ref[k:k + 1, lanes] * abuf_ref[SUBLANES * q:SUBLANES * q + sub + SUBLANES, lanes]
                    p_r = term if p_r is None else p_r + term
            acc = acc + p_r[r:r + sub, :]
        return acc

    def glu_conv(proj):
        abuf_ref[HALO:HALO + sub, :] = proj[:, 0:d_a] * jax.nn.sigmoid(proj[:, d_a:2 * d_a])
        conv = jnp.concatenate([conv_lane_block(i) for i in range(d_a // LANES)], axis=1)
        abuf_ref[0:HALO, :] = abuf_ref[sub:sub + HALO, :]
        return conv

    def finish(r, conv, proj):
        a_ln = _layernorm(conv, ag_ref[...], ab_ref[...])
        a_out = a_ln * jax.nn.sigmoid(a_ln)
        b_u = proj[:, 2 * d_a:2 * d_a + d_b]
        v = _layernorm(proj[:, 2 * d_a + d_b:], vg_ref[...], vb_ref[...]).astype(BF16)
        row = lax.broadcasted_iota(jnp.int32, (CHUNK, CHUNK), 0)
        col = lax.broadcasted_iota(jnp.int32, (CHUNK, CHUNK), 1)
        causal = row >= col
        mixed = []
        for hd in range(n_heads):
            lanes = slice(hd * HEAD_DIM_B, (hd + 1) * HEAD_DIM_B)
            rhs = jnp.concatenate([v[c * CHUNK:(c + 1) * CHUNK, lanes] for c in range(n_chunks)], axis=1)
            w_c = jnp.where(causal, ws_ref[hd], 0.0).astype(BF16)
            mixed.append(_bdot(w_c, rhs))
        bias = bias_ref[...]
        rows = []
        for c in range(n_chunks):
            lanes = slice(c * HEAD_DIM_B, (c + 1) * HEAD_DIM_B)
            rows.append(jnp.concatenate([mixed[hd][:, lanes] for hd in range(n_heads)], axis=1) + bias)
        b_out = b_u * jnp.concatenate(rows, axis=0)
        cat_ref[r * sub:(r + 1) * sub, :] = jnp.concatenate([a_out, b_out], axis=1).astype(BF16)

    proj = project(0)
    conv = glu_conv(proj)
    for r in range(1, x_ref.shape[0] // sub):
        proj_next = project(r)
        finish(r - 1, conv, proj)
        proj, conv = proj_next, glu_conv(proj_next)
    finish(x_ref.shape[0] // sub - 1, conv, proj)


N_OUT_PROJ_IN = 5


def _out_proj_kernel(*refs, sub, n_cast):
    cat_ref, x_ref, mod_ref, fg_ref, wout_ref = refs[:N_OUT_PROJ_IN]
    o_ref, h2_ref = refs[N_OUT_PROJ_IN + n_cast:N_OUT_PROJ_IN + n_cast + 2]
    _run_casts(refs[N_OUT_PROJ_IN:N_OUT_PROJ_IN + n_cast], refs[N_OUT_PROJ_IN + n_cast + 2:])
    m = mod_ref[...]
    gate = m[2:3]
    for r in range(0, cat_ref.shape[0], sub):
        rows = slice(r, r + sub)
        cat = cat_ref[rows, :]
        y = jnp.concatenate([_bdot(cat, wout_ref[c]) for c in range(wout_ref.shape[0])], axis=1)
        x_new = x_ref[rows, :] + gate * y
        o_ref[rows, :] = x_new
        h2_ref[rows, :] = _modulated_rmsnorm(x_new, fg_ref[...], m[3:4], m[4:5]).astype(BF16)


def _even_layer(x2, mod, norm_g, ffn_g, w_in, conv_w, conv_b, a_g, a_b, v_g, v_b, w_s, bias_full, w_out, l,
                mixer_casts=(), out_proj_casts=(), tm=256, sub=256, tm_out=512, sub_out=256):
    s, d = x2.shape
    i = l // 2
    d_a = conv_w.shape[2]
    d_b = v_g.shape[2]
    n_heads = w_s.shape[1]
    vec = lambda width, idx: pl.BlockSpec((None, 1, width), lambda t: (idx, 0, 0))
    jobs = [_row_cast_job(w, layer, s // tm, blocks) for w, layer, blocks in mixer_casts]
    outs = pl.pallas_call(
        functools.partial(_even_kernel, n_cast=len(jobs), sub=sub),
        grid=(s // tm,),
        in_specs=[
            pl.BlockSpec((tm, d), lambda t: (t, 0)),
            pl.BlockSpec((N_MOD, d), lambda t: (0, 0)),
            vec(d, l),
            _resident(w_in.shape, lambda t: (0, 0, 0)),
            pl.BlockSpec((None, CONV_WIDTH, d_a), lambda t: (i, 0, 0)),
            vec(d_a, i), vec(d_a, i), vec(d_a, i), vec(d_b, i), vec(d_b, i),
            pl.BlockSpec((None, n_heads, CHUNK, CHUNK), lambda t: (i, 0, 0, 0)),
            pl.BlockSpec((None, CHUNK, d_b), lambda t: (i, 0, 0)),
        ] + [job[1] for job in jobs],
        out_specs=[pl.BlockSpec((tm, d_a + d_b), lambda t: (t, 0))] + [job[2] for job in jobs],
        out_shape=[jax.ShapeDtypeStruct((s, d_a + d_b), BF16)] + [job[3] for job in jobs],
        scratch_shapes=[pltpu.VMEM((HALO + sub + SUBLANES, d_a), F32)],
        compiler_params=_params(("arbitrary",)),
        name="even_mixer",
    )(x2, mod, norm_g, w_in, conv_w, conv_b, a_g, a_b, v_g, v_b, w_s, bias_full, *[job[0] for job in jobs])
    cat, casts_mixer = outs[0], outs[1:]
    jobs = [_row_cast_job(w, layer, s // tm_out, blocks) for w, layer, blocks in out_proj_casts]
    outs = pl.pallas_call(
        functools.partial(_out_proj_kernel, sub=sub_out, n_cast=len(jobs)),
        grid=(s // tm_out,),
        in_specs=[
            pl.BlockSpec((tm_out, d_a + d_b), lambda t: (t, 0)),
            pl.BlockSpec((tm_out, d), lambda t: (t, 0)),
            pl.BlockSpec((N_MOD, d), lambda t: (0, 0)),
            vec(d, l),
            _resident(w_out.shape, lambda t: (0, 0, 0)),
        ] + [job[1] for job in jobs],
        out_specs=[pl.BlockSpec((tm_out, d), lambda t: (t, 0))] * 2 + [job[2] for job in jobs],
        out_shape=[jax.ShapeDtypeStruct((s, d), F32), jax.ShapeDtypeStruct((s, d), BF16)] + [job[3] for job in jobs],
        compiler_params=_params(("arbitrary",)),
        name="even_out_proj",
    )(cat, x2, mod, ffn_g, w_out, *[job[0] for job in jobs])
    return outs[0], outs[1], casts_mixer, outs[2:]


def _pool_kernel(x_ref, mod_ref, g_ref, fg_ref, pw_ref, ps_ref, o_ref, h2_ref, hbuf_ref, p_ref, q_ref):
    tm, d = x_ref.shape
    n_groups = len(POOL_WINDOWS)
    dg = d // n_groups
    t0 = pl.program_id(0) * tm

    @pl.when(pl.program_id(0) == 0)
    def _():
        hbuf_ref[0:HALO, :] = jnp.zeros((HALO, d), F32)

    x = x_ref[...]
    m = mod_ref[...]
    shift, scale, gate = m[0:1], m[1:2], m[2:3]
    h = _modulated_rmsnorm(x, g_ref[...], shift, scale)
    hbuf_ref[HALO:HALO + tm, :] = h

    end = HALO + tm
    p_ref[8:end, :] = hbuf_ref[8:end, :] + hbuf_ref[7:end - 1, :]
    q_ref[16:end, dg:] = p_ref[16:end, dg:] + p_ref[14:end - 2, dg:]
    p_ref[24:end, 2 * dg:] = q_ref[24:end, 2 * dg:] + q_ref[20:end - 4, 2 * dg:]
    q_ref[32:end, 3 * dg:] = p_ref[32:end, 3 * dg:] + p_ref[24:end - 8, 3 * dg:]
    sums = (p_ref, q_ref, p_ref, q_ref)

    pos = (lax.broadcasted_iota(jnp.int32, (tm, LANES), 0) + (t0 + 1)).astype(F32)
    ys = []
    for gi, w in enumerate(POOL_WINDOWS):
        lanes = slice(gi * dg, (gi + 1) * dg)
        inv_cnt = jnp.concatenate([1.0 / jnp.minimum(pos, float(w))] * (dg // LANES), axis=1)
        pooled = sums[gi][HALO:end, lanes] * inv_cnt - h[:, lanes]
        ys.append(_bdot(pooled.astype(BF16), pw_ref[gi]))
    y = jnp.concatenate(ys, axis=1) * ps_ref[...]
    _store_residual_and_ffn_input(x + gate * y, m, fg_ref, o_ref, h2_ref)
    hbuf_ref[0:HALO, :] = hbuf_ref[tm:tm + HALO, :]


def _pool_layer(x2, mod, norm_g, ffn_g, pool_w, pool_scale, l, tm=512):
    s, d = x2.shape
    i = l // 2
    n_groups, dg, _ = pool_w.shape[1:]
    return pl.pallas_call(
        _pool_kernel,
        grid=(s // tm,),
        in_specs=[
            pl.BlockSpec((tm, d), lambda t: (t, 0)),
            pl.BlockSpec((N_MOD, d), lambda t: (0, 0)),
            pl.BlockSpec((None, 1, d), lambda t: (l, 0, 0)),
            pl.BlockSpec((None, 1, d), lambda t: (l, 0, 0)),
            _resident((None, n_groups, dg, dg), lambda t: (i, 0, 0, 0)),
            pl.BlockSpec((None, 1, d), lambda t: (i, 0, 0)),
        ],
        out_specs=[pl.BlockSpec((tm, d), lambda t: (t, 0))] * 2,
        out_shape=[jax.ShapeDtypeStruct((s, d), F32), jax.ShapeDtypeStruct((s, d), BF16)],
        scratch_shapes=[pltpu.VMEM((HALO + tm, d), F32)] * 3,
        compiler_params=_params(("arbitrary",)),
        name="pool_mixer",
    )(x2, mod, norm_g, ffn_g, pool_w, pool_scale)


N_FFN_IN = 7


def _ffn_kernel(*refs, final_norm, n_cast, n_x, n_ada):
    h_ref, x_ref, mod_ref, w1_ref, w3_ref, w2_ref, fg_ref = refs[:N_FFN_IN]
    n_in = N_FFN_IN + (3 if n_ada else 0)
    cast_src = refs[n_in:n_in + n_cast]
    o_ref = refs[n_in + n_cast]
    n_out = 2 if n_ada else 1
    cast_dst = refs[n_in + n_cast + n_out:]
    j = pl.program_id(1)
    last = pl.num_programs(1) - 1
    x_rows = x_ref.shape[0]

    @pl.when(j == 0)
    def _():
        o_ref[...] = jnp.zeros(o_ref.shape, F32)

    if n_ada:
        ccol_ref, adaw_ref, adab_ref = refs[N_FFN_IN:n_in]
        modn_ref = refs[n_in + n_cast + 1]

        @pl.when(j == 0)
        def _():
            modn_ref[...] = adab_ref[...]

        ccol = ccol_ref[...]
        part = jnp.sum(ccol * jax.nn.sigmoid(ccol) * adaw_ref[...], axis=0, keepdims=True)
        modn_ref[...] += jnp.where(j < n_ada, part, 0.0)

    for src, dst in zip(cast_src, cast_dst):
        dst[...] = src[...].astype(BF16)

    h = h_ref[...]
    tf = w1_ref.shape[1]
    us = []
    for lanes in (slice(0, tf // 2), slice(tf // 2, tf)):
        a = _bdot(h, w1_ref[:, lanes])
        b = _bdot(h, w3_ref[:, lanes])
        us.append((a * jax.nn.sigmoid(a) * b).astype(BF16))
    u = jnp.concatenate(us, axis=1)
    y = mod_ref[5:6, :] * _bdot(u, w2_ref[...])
    for k in range(n_x):
        rows = slice(k * x_rows, (k + 1) * x_rows)
        o_ref[rows, :] += y[rows, :] + jnp.where(j == k, x_ref[...], 0.0)

    if final_norm:
        @pl.when(j == last)
        def _():
            out = o_ref[...]
            ms = jnp.mean(out * out, axis=-1, keepdims=True)
            o_ref[...] = out * lax.rsqrt(ms + EPS) * fg_ref[...]


def _cast_job(w, layer, n_i, n_j, column_blocks=False):
    rows, cols = w.shape[1:]
    if column_blocks:
        n_cb, rb = cols // W_BLOCK, rows // n_i
        assert n_cb <= n_j
        in_spec = pl.BlockSpec((None, rb, W_BLOCK), lambda t, j: (layer, t, jnp.minimum(j, n_cb - 1)))
        out_spec = pl.BlockSpec((None, rb, W_BLOCK), lambda t, j: (jnp.minimum(j, n_cb - 1), t, 0))
        return w, in_spec, out_spec, jax.ShapeDtypeStruct((n_cb, rows, W_BLOCK), BF16)
    if rows % (n_i * n_j * BF16_SUBLANES) == 0:
        rb = rows // (n_i * n_j)
        in_spec = pl.BlockSpec((None, rb, cols), lambda t, j: (layer, j * n_i + t, 0))
        out_spec = pl.BlockSpec((rb, cols), lambda t, j: (j * n_i + t, 0))
    else:
        n_cb = max(k for k in range(1, n_j + 1) if cols % (k * LANES) == 0)
        rb, cb = rows // n_i, cols // n_cb
        in_spec = pl.BlockSpec((None, rb, cb), lambda t, j: (layer, t, jnp.minimum(j, n_cb - 1)))
        out_spec = pl.BlockSpec((rb, cb), lambda t, j: (t, jnp.minimum(j, n_cb - 1)))
    return w, in_spec, out_spec, jax.ShapeDtypeStruct((rows, cols), BF16)


def _ffn_layer(h2, x2, mod, w1, w3, w2, final_g, final_norm, next_ada=None, cast_ahead=(), tm=1024, tf=512):
    s, d = x2.shape
    d_ff = w1.shape[1]
    n_i, n_j = s // tm, d_ff // tf
    n_x = max(k for k in range(1, n_j + 1) if tm % (k * SUBLANES) == 0)
    jobs = [_cast_job(w, layer, n_i, n_j, blocks) for w, layer, blocks in cast_ahead]
    ada_in, ada_specs, ada_out_specs, ada_out_shapes, n_ada = [], [], [], [], 0
    if next_ada is not None:
        c_cols, ada_w, ada_b3, layer = next_ada
        n_ada, kb, _ = c_cols.shape
        n_mod = ada_w.shape[2]
        cb = n_mod // n_i
        assert n_ada <= n_j and cb % LANES == 0
        kblk = lambda j: jnp.minimum(j, n_ada - 1)
        ada_in = [c_cols, ada_w, ada_b3]
        ada_specs = [
            pl.BlockSpec((None, kb, 1), lambda t, j: (kblk(j), 0, 0)),
            pl.BlockSpec((None, kb, cb), lambda t, j: (layer, kblk(j), t)),
            pl.BlockSpec((None, 1, cb), lambda t, j: (layer, 0, t)),
        ]
        ada_out_specs = [pl.BlockSpec((1, cb), lambda t, j: (0, t))]
        ada_out_shapes = [jax.ShapeDtypeStruct((1, n_mod), F32)]
    outs = pl.pallas_call(
        functools.partial(_ffn_kernel, final_norm=final_norm, n_cast=len(jobs), n_x=n_x, n_ada=n_ada),
        grid=(n_i, n_j),
        in_specs=[
            pl.BlockSpec((tm, d), lambda t, j: (t, 0)),
            pl.BlockSpec((tm // n_x, d), lambda t, j: (t * n_x + jnp.minimum(j, n_x - 1), 0)),
            pl.BlockSpec((N_MOD, d), lambda t, j: (0, 0)),
            pl.BlockSpec((d, tf), lambda t, j: (0, j)),
            pl.BlockSpec((d, tf), lambda t, j: (0, j)),
            pl.BlockSpec((tf, d), lambda t, j: (j, 0)),
            pl.BlockSpec((1, d), lambda t, j: (0, 0)),
        ] + ada_specs + [job[1] for job in jobs],
        out_specs=[pl.BlockSpec((tm, d), lambda t, j: (t, 0))] + ada_out_specs + [job[2] for job in jobs],
        out_shape=[jax.ShapeDtypeStruct((s, d), F32)] + ada_out_shapes + [job[3] for job in jobs],
        compiler_params=_params(("arbitrary", "arbitrary")),
        name="ffn",
    )(h2, x2, mod, w1, w3, w2, final_g, *ada_in, *[job[0] for job in jobs])
    n_fixed = 1 + len(ada_out_shapes)
    mod_next = outs[1].reshape(N_MOD, d) if next_ada is not None else None
    return outs[0], mod_next, outs[n_fixed:]


def kernel(x, c, ada_w, ada_b, norm_mix_g, norm_ffn_g, ab_w_in, a_conv_w, a_conv_b, a_norm_g, a_norm_b,
           b_norm_g, b_norm_b, b_w_s, b_bias, ab_w_out, pool_w, pool_scale, ffn_w1, ffn_w3, ffn_w2, final_g):
    bsz, s, d = x.shape
    depth = ada_w.shape[0]
    assert bsz == 1, "per-sequence adaLN terms are passed as single rows"
    row3 = lambda p: p.reshape(p.shape[0], 1, p.shape[1])

    ada_b3 = row3(ada_b)
    mod, (w_in, w_out) = _adaln_first(c, ada_w, ada_b3, [(ab_w_in, 0, True), (ab_w_out, 0, True)])
    c_cols = c.reshape(ADA_ROW_BLOCKS, d // ADA_ROW_BLOCKS, 1)
    n_even, n_heads, chunk = b_bias.shape
    bias_full = jnp.broadcast_to(
        jnp.transpose(b_bias, (0, 2, 1))[:, :, :, None], (n_even, chunk, n_heads, HEAD_DIM_B)
    ).reshape(n_even, chunk, n_heads * HEAD_DIM_B)

    p_w = pool_w.astype(BF16)
    w1 = w3 = w2 = None
    mix_g, ffn_g = row3(norm_mix_g), row3(norm_ffn_g)
    conv_b, a_g, a_b, v_g, v_b = (row3(p) for p in (a_conv_b, a_norm_g, a_norm_b, b_norm_g, b_norm_b))
    p_scale = row3(pool_scale)
    fg = final_g.reshape(1, d)

    x2 = x.reshape(s, d)
    for l in range(depth):
        if l % 2 == 0:
            first = l == 0
            x2, h2, casts_m, casts_o = _even_layer(
                x2, mod, mix_g, ffn_g, w_in, a_conv_w, conv_b, a_g, a_b, v_g, v_b, b_w_s, bias_full, w_out, l,
                mixer_casts=[(ffn_w2, 0, False)] if first else (),
                out_proj_casts=[(ffn_w1, 0, False), (ffn_w3, 0, False)] if first else ())
            if first:
                (w2,), (w1, w3) = casts_m, casts_o
        else:
            x2, h2 = _pool_layer(x2, mod, mix_g, ffn_g, p_w, p_scale, l)
        cast_ahead, next_ada = [], None
        if l + 1 < depth:
            next_ada = (c_cols, ada_w, ada_b3, l + 1)
            cast_ahead += [(ffn_w1, l + 1, False), (ffn_w3, l + 1, False), (ffn_w2, l + 1, False)]
            if (l + 1) % 2 == 0:
                cast_ahead += [(ab_w_in, (l + 1) // 2, True), (ab_w_out, (l + 1) // 2, True)]
        x2, mod, casts = _ffn_layer(h2, x2, mod, w1, w3, w2, fg, final_norm=(l == depth - 1),
                                    next_ada=next_ada, cast_ahead=cast_ahead)
        if casts:
            w1, w3, w2 = casts[:3]
            if len(casts) > 3:
                w_in, w_out = casts[3:]
    return x2.reshape(bsz, s, d)
```

```python
import functools

import jax
import jax.numpy as jnp
from jax import lax
from jax.experimental import pallas as pl
from jax.experimental.pallas import tpu as pltpu

F32 = jnp.float32
BF16 = jnp.bfloat16

EPS = 1e-6
CONV_WIDTH = 31
CHUNK = 128
HEAD_DIM_B = 128
POOL_WINDOWS = (2, 4, 8, 16)
N_MOD = 6

HALO = 32
SUBLANES = 8
BF16_SUBLANES = 16
LANES = 128
W_BLOCK = 512
ADA_ROW_BLOCKS = 8
VMEM_LIMIT_V7X = 56 * 1024 * 1024


def _params(semantics):
    return pltpu.CompilerParams(dimension_semantics=semantics, vmem_limit_bytes=VMEM_LIMIT_V7X)


def _resident(shape, index_map):
    return pl.BlockSpec(shape, index_map, pipeline_mode=pl.Buffered(1))


def _bdot(a, b):
    return jnp.dot(a, b, preferred_element_type=F32)


def _modulated_rmsnorm(x, gain, shift, scale):
    ms = jnp.mean(x * x, axis=-1, keepdims=True)
    return x * lax.rsqrt(ms + EPS) * (gain * (1.0 + scale)) + shift


def _layernorm(x, g, b):
    mu = jnp.mean(x, axis=-1, keepdims=True)
    xc = x - mu
    var = jnp.mean(xc * xc, axis=-1, keepdims=True)
    return xc * lax.rsqrt(var + EPS) * g + b


def _store_residual_and_ffn_input(x_new, m, fg_ref, o_ref, h2_ref):
    o_ref[...] = x_new
    h2_ref[...] = _modulated_rmsnorm(x_new, fg_ref[...], m[3:4], m[4:5]).astype(BF16)


def _row_cast_job(w, layer, n_steps, column_blocks=False):
    rows, cols = w.shape[1:]
    rb = rows // n_steps
    assert rb * n_steps == rows and rb % BF16_SUBLANES == 0
    in_spec = pl.BlockSpec((None, rb, cols), lambda t: (layer, t, 0))
    if column_blocks:
        n_cb = cols // W_BLOCK
        return (w, in_spec, pl.BlockSpec((n_cb, rb, W_BLOCK), lambda t: (0, t, 0)),
                jax.ShapeDtypeStruct((n_cb, rows, W_BLOCK), BF16))
    return w, in_spec, pl.BlockSpec((rb, cols), lambda t: (t, 0)), jax.ShapeDtypeStruct((rows, cols), BF16)


def _run_casts(cast_src, cast_dst):
    for src, dst in zip(cast_src, cast_dst):
        if len(dst.shape) == 3:
            for c in range(dst.shape[0]):
                dst[c] = src[:, c * W_BLOCK:(c + 1) * W_BLOCK].astype(BF16)
        else:
            dst[...] = src[...].astype(BF16)


def _adaln_kernel(*refs, n_cast):
    c_ref, w_ref, b_ref = refs[:3]
    o_ref = refs[3 + n_cast]
    c = c_ref[...]
    cond = c * jax.nn.sigmoid(c)
    cond8 = jnp.broadcast_to(cond, (8, cond.shape[1])).astype(BF16)
    y = _bdot(cond8, w_ref[...].astype(BF16))
    o_ref[...] = y[0:1, :] + b_ref[...]
    _run_casts(refs[3:3 + n_cast], refs[4 + n_cast:])


def _adaln_first(c, ada_w, ada_b3, cast_ahead=(), tn=1536):
    _, d, n = ada_w.shape
    jobs = [_row_cast_job(w, layer, n // tn, blocks) for w, layer, blocks in cast_ahead]
    outs = pl.pallas_call(
        functools.partial(_adaln_kernel, n_cast=len(jobs)),
        grid=(n // tn,),
        in_specs=[
            pl.BlockSpec((1, d), lambda j: (0, 0)),
            pl.BlockSpec((None, d, tn), lambda j: (0, 0, j)),
            pl.BlockSpec((None, 1, tn), lambda j: (0, 0, j)),
        ] + [job[1] for job in jobs],
        out_specs=[pl.BlockSpec((1, tn), lambda j: (0, j))] + [job[2] for job in jobs],
        out_shape=[jax.ShapeDtypeStruct((1, n), F32)] + [job[3] for job in jobs],
        compiler_params=_params(("arbitrary",)),
        name="adaln",
    )(c, ada_w, ada_b3, *[job[0] for job in jobs])
    return outs[0].reshape(N_MOD, d), outs[1:]


N_EVEN_IN = 12


def _even_kernel(*refs, n_cast, sub):
    (x_ref, mod_ref, g_ref, win_ref, cw_ref, cb_ref, ag_ref, ab_ref, vg_ref, vb_ref,
     ws_ref, bias_ref) = refs[:N_EVEN_IN]
    cat_ref, abuf_ref = refs[N_EVEN_IN + n_cast], refs[-1]
    _run_casts(refs[N_EVEN_IN:N_EVEN_IN + n_cast], refs[N_EVEN_IN + n_cast + 1:-1])
    d_a = cw_ref.shape[1]
    n_heads = ws_ref.shape[0]
    n_chunks = sub // CHUNK
    d_b = n_heads * HEAD_DIM_B

    @pl.when(pl.program_id(0) == 0)
    def _():
        abuf_ref[0:HALO, :] = jnp.zeros((HALO, d_a), F32)
        abuf_ref[HALO + sub:HALO + sub + SUBLANES, :] = jnp.zeros((SUBLANES, d_a), F32)

    m = mod_ref[...]
    shift, scale = m[0:1], m[1:2]

    def project(r):
        h = _modulated_rmsnorm(x_ref[r * sub:(r + 1) * sub, :], g_ref[...], shift, scale).astype(BF16)
        return jnp.concatenate([_bdot(h, win_ref[c]) for c in range(win_ref.shape[0])], axis=1)

    first = HALO - (CONV_WIDTH - 1)

    def conv_lane_block(i):
        lanes = slice(i * LANES, (i + 1) * LANES)
        acc = cb_ref[:, lanes]
        for r in range(SUBLANES):
            p_r = None
            for q in range((first + CONV_WIDTH - 1) // SUBLANES + 1):
                k = SUBLANES * q + r - first
                if 0 <= k < CONV_WIDTH:
                    term = cw_ref[k:k + 1, lanes] * abuf_ref[SUBLANES * q:SUBLANES * q + sub + SUBLANES, lanes]
                    p_r = term if p_r is None else p_r + term
            acc = acc + p_r[r:r + sub, :]
        return acc

    def glu_conv(proj):
        abuf_ref[HALO:HALO + sub, :] = proj[:, 0:d_a] * jax.nn.sigmoid(proj[:, d_a:2 * d_a])
        conv = jnp.concatenate([conv_lane_block(i) for i in range(d_a // LANES)], axis=1)
        abuf_ref[0:HALO, :] = abuf_ref[sub:sub + HALO, :]
        return conv

    def finish(r, conv, proj):
        a_ln = _layernorm(conv, ag_ref[...], ab_ref[...])
        a_out = a_ln * jax.nn.sigmoid(a_ln)
        b_u = proj[:, 2 * d_a:2 * d_a + d_b]
        v = _layernorm(proj[:, 2 * d_a + d_b:], vg_ref[...], vb_ref[...]).astype(BF16)
        row = lax.broadcasted_iota(jnp.int32, (CHUNK, CHUNK), 0)
        col = lax.broadcasted_iota(jnp.int32, (CHUNK, CHUNK), 1)
        causal = row >= col
        mixed = []
        for hd in range(n_heads):
            lanes = slice(hd * HEAD_DIM_B, (hd + 1) * HEAD_DIM_B)
            rhs = jnp.concatenate([v[c * CHUNK:(c + 1) * CHUNK, lanes] for c in range(n_chunks)], axis=1)
            w_c = jnp.where(causal, ws_ref[hd], 0.0).astype(BF16)
            mixed.append(_bdot(w_c, rhs))
        bias = bias_ref[...]
        rows = []
        for c in range(n_chunks):
            lanes = slice(c * HEAD_DIM_B, (c + 1) * HEAD_DIM_B)
            rows.append(jnp.concatenate([mixed[hd][:, lanes] for hd in range(n_heads)], axis=1) + bias)
        b_out = b_u * jnp.concatenate(rows, axis=0)
        cat_ref[r * sub:(r + 1) * sub, :] = jnp.concatenate([a_out, b_out], axis=1).astype(BF16)

    proj = project(0)
    conv = glu_conv(proj)
    for r in range(1, x_ref.shape[0] // sub):
        proj_next = project(r)
        finish(r - 1, conv, proj)
        proj, conv = proj_next, glu_conv(proj_next)
    finish(x_ref.shape[0] // sub - 1, conv, proj)


def _out_proj_kernel(cat_ref, x_ref, mod_ref, fg_ref, wout_ref, o_ref, h2_ref, *, sub):
    m = mod_ref[...]
    gate = m[2:3]
    for r in range(0, cat_ref.shape[0], sub):
        rows = slice(r, r + sub)
        cat = cat_ref[rows, :]
        y = jnp.concatenate([_bdot(cat, wout_ref[c]) for c in range(wout_ref.shape[0])], axis=1)
        x_new = x_ref[rows, :] + gate * y
        o_ref[rows, :] = x_new
        h2_ref[rows, :] = _modulated_rmsnorm(x_new, fg_ref[...], m[3:4], m[4:5]).astype(BF16)


def _even_layer(x2, mod, norm_g, ffn_g, w_in, conv_w, conv_b, a_g, a_b, v_g, v_b, w_s, bias_full, w_out, l,
                mixer_casts=(), tm=256, sub=256, tm_out=512, sub_out=256):
    s, d = x2.shape
    i = l // 2
    d_a = conv_w.shape[2]
    d_b = v_g.shape[2]
    n_heads = w_s.shape[1]
    vec = lambda width, idx: pl.BlockSpec((None, 1, width), lambda t: (idx, 0, 0))
    jobs = [_row_cast_job(w, layer, s // tm, blocks) for w, layer, blocks in mixer_casts]
    outs = pl.pallas_call(
        functools.partial(_even_kernel, n_cast=len(jobs), sub=sub),
        grid=(s // tm,),
        in_specs=[
            pl.BlockSpec((tm, d), lambda t: (t, 0)),
            pl.BlockSpec((N_MOD, d), lambda t: (0, 0)),
            vec(d, l),
            _resident(w_in.shape, lambda t: (0, 0, 0)),
            pl.BlockSpec((None, CONV_WIDTH, d_a), lambda t: (i, 0, 0)),
            vec(d_a, i), vec(d_a, i), vec(d_a, i), vec(d_b, i), vec(d_b, i),
            pl.BlockSpec((None, n_heads, CHUNK, CHUNK), lambda t: (i, 0, 0, 0)),
            pl.BlockSpec((None, CHUNK, d_b), lambda t: (i, 0, 0)),
        ] + [job[1] for job in jobs],
        out_specs=[pl.BlockSpec((tm, d_a + d_b), lambda t: (t, 0))] + [job[2] for job in jobs],
        out_shape=[jax.ShapeDtypeStruct((s, d_a + d_b), BF16)] + [job[3] for job in jobs],
        scratch_shapes=[pltpu.VMEM((HALO + sub + SUBLANES, d_a), F32)],
        compiler_params=_params(("arbitrary",)),
        name="even_mixer",
    )(x2, mod, norm_g, w_in, conv_w, conv_b, a_g, a_b, v_g, v_b, w_s, bias_full, *[job[0] for job in jobs])
    cat, casts = outs[0], outs[1:]
    x_new, h2 = pl.pallas_call(
        functools.partial(_out_proj_kernel, sub=sub_out),
        grid=(s // tm_out,),
        in_specs=[
            pl.BlockSpec((tm_out, d_a + d_b), lambda t: (t, 0)),
            pl.BlockSpec((tm_out, d), lambda t: (t, 0)),
            pl.BlockSpec((N_MOD, d), lambda t: (0, 0)),
            vec(d, l),
            _resident(w_out.shape, lambda t: (0, 0, 0)),
        ],
        out_specs=[pl.BlockSpec((tm_out, d), lambda t: (t, 0))] * 2,
        out_shape=[jax.ShapeDtypeStruct((s, d), F32), jax.ShapeDtypeStruct((s, d), BF16)],
        compiler_params=_params(("arbitrary",)),
        name="even_out_proj",
    )(cat, x2, mod, ffn_g, w_out)
    return x_new, h2, casts


def _pool_kernel(x_ref, mod_ref, g_ref, fg_ref, pw_ref, ps_ref, o_ref, h2_ref, hbuf_ref, p_ref, q_ref):
    tm, d = x_ref.shape
    n_groups = len(POOL_WINDOWS)
    dg = d // n_groups
    t0 = pl.program_id(0) * tm

    @pl.when(pl.program_id(0) == 0)
    def _():
        hbuf_ref[0:HALO, :] = jnp.zeros((HALO, d), F32)

    x = x_ref[...]
    m = mod_ref[...]
    shift, scale, gate = m[0:1], m[1:2], m[2:3]
    h = _modulated_rmsnorm(x, g_ref[...], shift, scale)
    hbuf_ref[HALO:HALO + tm, :] = h

    end = HALO + tm
    p_ref[8:end, :] = hbuf_ref[8:end, :] + hbuf_ref[7:end - 1, :]
    q_ref[16:end, dg:] = p_ref[16:end, dg:] + p_ref[14:end - 2, dg:]
    p_ref[24:end, 2 * dg:] = q_ref[24:end, 2 * dg:] + q_ref[20:end - 4, 2 * dg:]
    q_ref[32:end, 3 * dg:] = p_ref[32:end, 3 * dg:] + p_ref[24:end - 8, 3 * dg:]
    sums = (p_ref, q_ref, p_ref, q_ref)

    pos = (lax.broadcasted_iota(jnp.int32, (tm, LANES), 0) + (t0 + 1)).astype(F32)
    ys = []
    for gi, w in enumerate(POOL_WINDOWS):
        lanes = slice(gi * dg, (gi + 1) * dg)
        inv_cnt = jnp.concatenate([1.0 / jnp.minimum(pos, float(w))] * (dg // LANES), axis=1)
        pooled = sums[gi][HALO:end, lanes] * inv_cnt - h[:, lanes]
        ys.append(_bdot(pooled.astype(BF16), pw_ref[gi]))
    y = jnp.concatenate(ys, axis=1) * ps_ref[...]
    _store_residual_and_ffn_input(x + gate * y, m, fg_ref, o_ref, h2_ref)
    hbuf_ref[0:HALO, :] = hbuf_ref[tm:tm + HALO, :]


def _pool_layer(x2, mod, norm_g, ffn_g, pool_w, pool_scale, l, tm=512):
    s, d = x2.shape
    i = l // 2
    n_groups, dg, _ = pool_w.shape[1:]
    return pl.pallas_call(
        _pool_kernel,
        grid=(s // tm,),
        in_specs=[
            pl.BlockSpec((tm, d), lambda t: (t, 0)),
            pl.BlockSpec((N_MOD, d), lambda t: (0, 0)),
            pl.BlockSpec((None, 1, d), lambda t: (l, 0, 0)),
            pl.BlockSpec((None, 1, d), lambda t: (l, 0, 0)),
            _resident((None, n_groups, dg, dg), lambda t: (i, 0, 0, 0)),
            pl.BlockSpec((None, 1, d), lambda t: (i, 0, 0)),
        ],
        out_specs=[pl.BlockSpec((tm, d), lambda t: (t, 0))] * 2,
        out_shape=[jax.ShapeDtypeStruct((s, d), F32), jax.ShapeDtypeStruct((s, d), BF16)],
        scratch_shapes=[pltpu.VMEM((HALO + tm, d), F32)] * 3,
        compiler_params=_params(("arbitrary",)),
        name="pool_mixer",
    )(x2, mod, norm_g, ffn_g, pool_w, pool_scale)


N_FFN_IN = 7


def _ffn_kernel(*refs, final_norm, n_cast, n_x, n_ada):
    h_ref, x_ref, mod_ref, w1_ref, w3_ref, w2_ref, fg_ref = refs[:N_FFN_IN]
    n_in = N_FFN_IN + (3 if n_ada else 0)
    cast_src = refs[n_in:n_in + n_cast]
    o_ref = refs[n_in + n_cast]
    n_out = 2 if n_ada else 1
    cast_dst = refs[n_in + n_cast + n_out:]
    j = pl.program_id(1)
    last = pl.num_programs(1) - 1
    x_rows = x_ref.shape[0]

    @pl.when(j == 0)
    def _():
        o_ref[...] = jnp.zeros(o_ref.shape, F32)

    if n_ada:
        ccol_ref, adaw_ref, adab_ref = refs[N_FFN_IN:n_in]
        modn_ref = refs[n_in + n_cast + 1]

        @pl.when(j == 0)
        def _():
            modn_ref[...] = adab_ref[...]

        ccol = ccol_ref[...]
        part = jnp.sum(ccol * jax.nn.sigmoid(ccol) * adaw_ref[...], axis=0, keepdims=True)
        modn_ref[...] += jnp.where(j < n_ada, part, 0.0)

    for src, dst in zip(cast_src, cast_dst):
        dst[...] = src[...].astype(BF16)

    h = h_ref[...]
    tf = w1_ref.shape[1]
    us = []
    for lanes in (slice(0, tf // 2), slice(tf // 2, tf)):
        a = _bdot(h, w1_ref[:, lanes])
        b = _bdot(h, w3_ref[:, lanes])
        us.append((a * jax.nn.sigmoid(a) * b).astype(BF16))
    u = jnp.concatenate(us, axis=1)
    y = mod_ref[5:6, :] * _bdot(u, w2_ref[...])
    for k in range(n_x):
        rows = slice(k * x_rows, (k + 1) * x_rows)
        o_ref[rows, :] += y[rows, :] + jnp.where(j == k, x_ref[...], 0.0)

    if final_norm:
        @pl.when(j == last)
        def _():
            out = o_ref[...]
            ms = jnp.mean(out * out, axis=-1, keepdims=True)
            o_ref[...] = out * lax.rsqrt(ms + EPS) * fg_ref[...]


def _cast_job(w, layer, n_i, n_j, column_blocks=False):
    rows, cols = w.shape[1:]
    if column_blocks:
        n_cb, rb = cols // W_BLOCK, rows // n_i
        assert n_cb <= n_j
        in_spec = pl.BlockSpec((None, rb, W_BLOCK), lambda t, j: (layer, t, jnp.minimum(j, n_cb - 1)))
        out_spec = pl.BlockSpec((None, rb, W_BLOCK), lambda t, j: (jnp.minimum(j, n_cb - 1), t, 0))
        return w, in_spec, out_spec, jax.ShapeDtypeStruct((n_cb, rows, W_BLOCK), BF16)
    if rows % (n_i * n_j * BF16_SUBLANES) == 0:
        rb = rows // (n_i * n_j)
        in_spec = pl.BlockSpec((None, rb, cols), lambda t, j: (layer, j * n_i + t, 0))
        out_spec = pl.BlockSpec((rb, cols), lambda t, j: (j * n_i + t, 0))
    else:
        n_cb = max(k for k in range(1, n_j + 1) if cols % (k * LANES) == 0)
        rb, cb = rows // n_i, cols // n_cb
        in_spec = pl.BlockSpec((None, rb, cb), lambda t, j: (layer, t, jnp.minimum(j, n_cb - 1)))
        out_spec = pl.BlockSpec((rb, cb), lambda t, j: (t, jnp.minimum(j, n_cb - 1)))
    return w, in_spec, out_spec, jax.ShapeDtypeStruct((rows, cols), BF16)


def _ffn_layer(h2, x2, mod, w1, w3, w2, final_g, final_norm, next_ada=None, cast_ahead=(), tm=1024, tf=512):
    s, d = x2.shape
    d_ff = w1.shape[1]
    n_i, n_j = s // tm, d_ff // tf
    n_x = max(k for k in range(1, n_j + 1) if tm % (k * SUBLANES) == 0)
    jobs = [_cast_job(w, layer, n_i, n_j, blocks) for w, layer, blocks in cast_ahead]
    ada_in, ada_specs, ada_out_specs, ada_out_shapes, n_ada = [], [], [], [], 0
    if next_ada is not None:
        c_cols, ada_w, ada_b3, layer = next_ada
        n_ada, kb, _ = c_cols.shape
        n_mod = ada_w.shape[2]
        cb = n_mod // n_i
        assert n_ada <= n_j and cb % LANES == 0
        kblk = lambda j: jnp.minimum(j, n_ada - 1)
        ada_in = [c_cols, ada_w, ada_b3]
        ada_specs = [
            pl.BlockSpec((None, kb, 1), lambda t, j: (kblk(j), 0, 0)),
            pl.BlockSpec((None, kb, cb), lambda t, j: (layer, kblk(j), t)),
            pl.BlockSpec((None, 1, cb), lambda t, j: (layer, 0, t)),
        ]
        ada_out_specs = [pl.BlockSpec((1, cb), lambda t, j: (0, t))]
        ada_out_shapes = [jax.ShapeDtypeStruct((1, n_mod), F32)]
    outs = pl.pallas_call(
        functools.partial(_ffn_kernel, final_norm=final_norm, n_cast=len(jobs), n_x=n_x, n_ada=n_ada),
        grid=(n_i, n_j),
        in_specs=[
            pl.BlockSpec((tm, d), lambda t, j: (t, 0)),
            pl.BlockSpec((tm // n_x, d), lambda t, j: (t * n_x + jnp.minimum(j, n_x - 1), 0)),
            pl.BlockSpec((N_MOD, d), lambda t, j: (0, 0)),
            pl.BlockSpec((d, tf), lambda t, j: (0, j)),
            pl.BlockSpec((d, tf), lambda t, j: (0, j)),
            pl.BlockSpec((tf, d), lambda t, j: (j, 0)),
            pl.BlockSpec((1, d), lambda t, j: (0, 0)),
        ] + ada_specs + [job[1] for job in jobs],
        out_specs=[pl.BlockSpec((tm, d), lambda t, j: (t, 0))] + ada_out_specs + [job[2] for job in jobs],
        out_shape=[jax.ShapeDtypeStruct((s, d), F32)] + ada_out_shapes + [job[3] for job in jobs],
        compiler_params=_params(("arbitrary", "arbitrary")),
        name="ffn",
    )(h2, x2, mod, w1, w3, w2, final_g, *ada_in, *[job[0] for job in jobs])
    n_fixed = 1 + len(ada_out_shapes)
    mod_next = outs[1].reshape(N_MOD, d) if next_ada is not None else None
    return outs[0], mod_next, outs[n_fixed:]


def kernel(x, c, ada_w, ada_b, norm_mix_g, norm_ffn_g, ab_w_in, a_conv_w, a_conv_b, a_norm_g, a_norm_b,
           b_norm_g, b_norm_b, b_w_s, b_bias, ab_w_out, pool_w, pool_scale, ffn_w1, ffn_w3, ffn_w2, final_g):
    bsz, s, d = x.shape
    depth = ada_w.shape[0]
    assert bsz == 1, "per-sequence adaLN terms are passed as single rows"
    row3 = lambda p: p.reshape(p.shape[0], 1, p.shape[1])

    ada_b3 = row3(ada_b)
    mod, (w_in, w_out) = _adaln_first(c, ada_w, ada_b3, [(ab_w_in, 0, True), (ab_w_out, 0, True)])
    c_cols = c.reshape(ADA_ROW_BLOCKS, d // ADA_ROW_BLOCKS, 1)
    n_even, n_heads, chunk = b_bias.shape
    bias_full = jnp.broadcast_to(
        jnp.transpose(b_bias, (0, 2, 1))[:, :, :, None], (n_even, chunk, n_heads, HEAD_DIM_B)
    ).reshape(n_even, chunk, n_heads * HEAD_DIM_B)

    p_w = pool_w.astype(BF16)
    w1 = w3 = w2 = None
    mix_g, ffn_g = row3(norm_mix_g), row3(norm_ffn_g)
    conv_b, a_g, a_b, v_g, v_b = (row3(p) for p in (a_conv_b, a_norm_g, a_norm_b, b_norm_g, b_norm_b))
    p_scale = row3(pool_scale)
    fg = final_g.reshape(1, d)

    x2 = x.reshape(s, d)
    for l in range(depth):
        if l % 2 == 0:
            first = l == 0
            x2, h2, casts_m = _even_layer(
                x2, mod, mix_g, ffn_g, w_in, a_conv_w, conv_b, a_g, a_b, v_g, v_b, b_w_s, bias_full, w_out, l,
                mixer_casts=[(ffn_w1, 0, False), (ffn_w3, 0, False), (ffn_w2, 0, False)] if first else ())
            if first:
                w1, w3, w2 = casts_m
        else:
            x2, h2 = _pool_layer(x2, mod, mix_g, ffn_g, p_w, p_scale, l)
        cast_ahead, next_ada = [], None
        if l + 1 < depth:
            next_ada = (c_cols, ada_w, ada_b3, l + 1)
            cast_ahead += [(ffn_w1, l + 1, False), (ffn_w3, l + 1, False), (ffn_w2, l + 1, False)]
            if (l + 1) % 2 == 0:
                cast_ahead += [(ab_w_in, (l + 1) // 2, True), (ab_w_out, (l + 1) // 2, True)]
        x2, mod, casts = _ffn_layer(h2, x2, mod, w1, w3, w2, fg, final_norm=(l == depth - 1),
                                    next_ada=next_ada, cast_ahead=cast_ahead)
        if casts:
            w1, w3, w2 = casts[:3]
            if len(casts) > 3:
                w_in, w_out = casts[3:]
    return x2.reshape(bsz, s, d)
```

```python
import functools

import jax
import jax.numpy as jnp
from jax import lax
from jax.experimental import pallas as pl
from jax.experimental.pallas import tpu as pltpu

F32 = jnp.float32
BF16 = jnp.bfloat16

EPS = 1e-6
CONV_WIDTH = 31
CHUNK = 128
HEAD_DIM_B = 128
POOL_WINDOWS = (2, 4, 8, 16)
N_MOD = 6

HALO = 32
SUBLANES = 8
BF16_SUBLANES = 16
LANES = 128
W_BLOCK = 512
ADA_ROW_BLOCKS = 8
VMEM_LIMIT_V7X = 56 * 1024 * 1024


def _params(semantics):
    return pltpu.CompilerParams(dimension_semantics=semantics, vmem_limit_bytes=VMEM_LIMIT_V7X)


def _resident(shape, index_map):
    return pl.BlockSpec(shape, index_map, pipeline_mode=pl.Buffered(1))


def _bdot(a, b):
    return jnp.dot(a, b, preferred_element_type=F32)


def _modulated_rmsnorm(x, gain, shift, scale):
    ms = jnp.mean(x * x, axis=-1, keepdims=True)
    return x * lax.rsqrt(ms + EPS) * (gain * (1.0 + scale)) + shift


def _layernorm(x, g, b):
    mu = jnp.mean(x, axis=-1, keepdims=True)
    xc = x - mu
    var = jnp.mean(xc * xc, axis=-1, keepdims=True)
    return xc * lax.rsqrt(var + EPS) * g + b


def _mod_term(m, k):
    d = m.shape[1] // N_MOD
    return m[:, k * d:(k + 1) * d]


def _row(ref, idx):
    return ref[idx:idx + 1, :]


def _store_residual_and_ffn_input(x_new, m, ffn_gain, o_ref, h2_ref):
    o_ref[...] = x_new
    h2_ref[...] = _modulated_rmsnorm(x_new, ffn_gain, _mod_term(m, 3), _mod_term(m, 4)).astype(BF16)


def _row_cast_job(w, layer, n_steps, column_blocks=False):
    rows, cols = w.shape[1:]
    rb = rows // n_steps
    assert rb * n_steps == rows and rb % BF16_SUBLANES == 0
    in_spec = pl.BlockSpec((None, rb, cols), lambda t: (layer, t, 0))
    if column_blocks:
        n_cb = cols // W_BLOCK
        return (w, in_spec, pl.BlockSpec((n_cb, rb, W_BLOCK), lambda t: (0, t, 0)),
                jax.ShapeDtypeStruct((n_cb, rows, W_BLOCK), BF16))
    return w, in_spec, pl.BlockSpec((rb, cols), lambda t: (t, 0)), jax.ShapeDtypeStruct((rows, cols), BF16)


def _run_casts(cast_src, cast_dst):
    for src, dst in zip(cast_src, cast_dst):
        if len(dst.shape) == 3:
            for c in range(dst.shape[0]):
                dst[c] = src[:, c * W_BLOCK:(c + 1) * W_BLOCK].astype(BF16)
        else:
            dst[...] = src[...].astype(BF16)


def _adaln_kernel(*refs, n_cast):
    c_ref, w_ref, b_ref = refs[:3]
    o_ref = refs[3 + n_cast]
    c = c_ref[...]
    cond = c * jax.nn.sigmoid(c)
    cond8 = jnp.broadcast_to(cond, (8, cond.shape[1])).astype(BF16)
    y = _bdot(cond8, w_ref[...].astype(BF16))
    o_ref[...] = y[0:1, :] + _row(b_ref, 0)
    _run_casts(refs[3:3 + n_cast], refs[4 + n_cast:])


def _adaln_first(c, ada_w, ada_b, cast_ahead=(), tn=1536):
    depth, d, n = ada_w.shape
    jobs = [_row_cast_job(w, layer, n // tn, blocks) for w, layer, blocks in cast_ahead]
    outs = pl.pallas_call(
        functools.partial(_adaln_kernel, n_cast=len(jobs)),
        grid=(n // tn,),
        in_specs=[
            pl.BlockSpec((1, d), lambda j: (0, 0)),
            pl.BlockSpec((None, d, tn), lambda j: (0, 0, j)),
            pl.BlockSpec((depth, tn), lambda j: (0, j)),
        ] + [job[1] for job in jobs],
        out_specs=[pl.BlockSpec((1, tn), lambda j: (0, j))] + [job[2] for job in jobs],
        out_shape=[jax.ShapeDtypeStruct((1, n), F32)] + [job[3] for job in jobs],
        compiler_params=_params(("arbitrary",)),
        name="adaln",
    )(c, ada_w, ada_b, *[job[0] for job in jobs])
    return outs[0], outs[1:]


N_EVEN_IN = 12


def _even_kernel(*refs, n_cast, sub, layer, group):
    (x_ref, mod_ref, g_ref, win_ref, cw_ref, cb_ref, ag_ref, ab_ref, vg_ref, vb_ref,
     ws_ref, bias_ref) = refs[:N_EVEN_IN]
    cat_ref, abuf_ref = refs[N_EVEN_IN + n_cast], refs[-1]
    _run_casts(refs[N_EVEN_IN:N_EVEN_IN + n_cast], refs[N_EVEN_IN + n_cast + 1:-1])
    d_a = cw_ref.shape[1]
    n_heads = ws_ref.shape[0]
    n_chunks = sub // CHUNK
    d_b = n_heads * HEAD_DIM_B

    @pl.when(pl.program_id(0) == 0)
    def _():
        abuf_ref[0:HALO, :] = jnp.zeros((HALO, d_a), F32)
        abuf_ref[HALO + sub:HALO + sub + SUBLANES, :] = jnp.zeros((SUBLANES, d_a), F32)

    m = mod_ref[...]
    gain, shift, scale = _row(g_ref, layer), _mod_term(m, 0), _mod_term(m, 1)

    def project(r):
        h = _modulated_rmsnorm(x_ref[r * sub:(r + 1) * sub, :], gain, shift, scale).astype(BF16)
        return jnp.concatenate([_bdot(h, win_ref[c]) for c in range(win_ref.shape[0])], axis=1)

    first = HALO - (CONV_WIDTH - 1)

    def conv_lane_block(i):
        lanes = slice(i * LANES, (i + 1) * LANES)
        acc = cb_ref[group:group + 1, lanes]
        for r in range(SUBLANES):
            p_r = None
            for q in range((first + CONV_WIDTH - 1) // SUBLANES + 1):
                k = SUBLANES * q + r - first
                if 0 <= k < CONV_WIDTH:
                    term = cw_ref[k:k + 1, lanes] * abuf_ref[SUBLANES * q:SUBLANES * q + sub + SUBLANES, lanes]
                    p_r = term if p_r is None else p_r + term
            acc = acc + p_r[r:r + sub, :]
        return acc

    def glu_conv(proj):
        abuf_ref[HALO:HALO + sub, :] = proj[:, 0:d_a] * jax.nn.sigmoid(proj[:, d_a:2 * d_a])
        conv = jnp.concatenate([conv_lane_block(i) for i in range(d_a // LANES)], axis=1)
        abuf_ref[0:HALO, :] = abuf_ref[sub:sub + HALO, :]
        return conv

    def finish(r, conv, proj):
        a_ln = _layernorm(conv, _row(ag_ref, group), _row(ab_ref, group))
        a_out = a_ln * jax.nn.sigmoid(a_ln)
        b_u = proj[:, 2 * d_a:2 * d_a + d_b]
        v = _layernorm(proj[:, 2 * d_a + d_b:], _row(vg_ref, group), _row(vb_ref, group)).astype(BF16)
        row = lax.broadcasted_iota(jnp.int32, (CHUNK, CHUNK), 0)
        col = lax.broadcasted_iota(jnp.int32, (CHUNK, CHUNK), 1)
        causal = row >= col
        mixed = []
        for hd in range(n_heads):
            lanes = slice(hd * HEAD_DIM_B, (hd + 1) * HEAD_DIM_B)
            rhs = jnp.concatenate([v[c * CHUNK:(c + 1) * CHUNK, lanes] for c in range(n_chunks)], axis=1)
            w_c = jnp.where(causal, ws_ref[hd], 0.0).astype(BF16)
            mixed.append(_bdot(w_c, rhs))
        bias = bias_ref[...]
        rows = []
        for c in range(n_chunks):
            lanes = slice(c * HEAD_DIM_B, (c + 1) * HEAD_DIM_B)
            rows.append(jnp.concatenate([mixed[hd][:, lanes] for hd in range(n_heads)], axis=1) + bias)
        b_out = b_u * jnp.concatenate(rows, axis=0)
        cat_ref[r * sub:(r + 1) * sub, :] = jnp.concatenate([a_out, b_out], axis=1).astype(BF16)

    proj = project(0)
    conv = glu_conv(proj)
    for r in range(1, x_ref.shape[0] // sub):
        proj_next = project(r)
        finish(r - 1, conv, proj)
        proj, conv = proj_next, glu_conv(proj_next)
    finish(x_ref.shape[0] // sub - 1, conv, proj)


def _out_proj_kernel(cat_ref, x_ref, mod_ref, fg_ref, wout_ref, o_ref, h2_ref, *, sub, layer):
    m = mod_ref[...]
    gate, ffn_gain = _mod_term(m, 2), _row(fg_ref, layer)
    for r in range(0, cat_ref.shape[0], sub):
        rows = slice(r, r + sub)
        cat = cat_ref[rows, :]
        y = jnp.concatenate([_bdot(cat, wout_ref[c]) for c in range(wout_ref.shape[0])], axis=1)
        x_new = x_ref[rows, :] + gate * y
        o_ref[rows, :] = x_new
        h2_ref[rows, :] = _modulated_rmsnorm(x_new, ffn_gain, _mod_term(m, 3), _mod_term(m, 4)).astype(BF16)


def _even_layer(x2, mod, norm_g, ffn_g, w_in, conv_w, conv_b, a_g, a_b, v_g, v_b, w_s, bias_full, w_out, l,
                mixer_casts=(), tm=256, sub=256, tm_out=512, sub_out=256):
    s, d = x2.shape
    i = l // 2
    d_a = conv_w.shape[2]
    d_b = v_g.shape[1]
    n_heads = w_s.shape[1]
    stack = lambda p: pl.BlockSpec(p.shape, lambda t: (0, 0))
    jobs = [_row_cast_job(w, layer, s // tm, blocks) for w, layer, blocks in mixer_casts]
    outs = pl.pallas_call(
        functools.partial(_even_kernel, n_cast=len(jobs), sub=sub, layer=l, group=i),
        grid=(s // tm,),
        in_specs=[
            pl.BlockSpec((tm, d), lambda t: (t, 0)),
            stack(mod),
            stack(norm_g),
            _resident(w_in.shape, lambda t: (0, 0, 0)),
            pl.BlockSpec((None, CONV_WIDTH, d_a), lambda t: (i, 0, 0)),
            stack(conv_b), stack(a_g), stack(a_b), stack(v_g), stack(v_b),
            pl.BlockSpec((None, n_heads, CHUNK, CHUNK), lambda t: (i, 0, 0, 0)),
            pl.BlockSpec((None, CHUNK, d_b), lambda t: (i, 0, 0)),
        ] + [job[1] for job in jobs],
        out_specs=[pl.BlockSpec((tm, d_a + d_b), lambda t: (t, 0))] + [job[2] for job in jobs],
        out_shape=[jax.ShapeDtypeStruct((s, d_a + d_b), BF16)] + [job[3] for job in jobs],
        scratch_shapes=[pltpu.VMEM((HALO + sub + SUBLANES, d_a), F32)],
        compiler_params=_params(("arbitrary",)),
        name="even_mixer",
    )(x2, mod, norm_g, w_in, conv_w, conv_b, a_g, a_b, v_g, v_b, w_s, bias_full, *[job[0] for job in jobs])
    cat, casts = outs[0], outs[1:]
    x_new, h2 = pl.pallas_call(
        functools.partial(_out_proj_kernel, sub=sub_out, layer=l),
        grid=(s // tm_out,),
        in_specs=[
            pl.BlockSpec((tm_out, d_a + d_b), lambda t: (t, 0)),
            pl.BlockSpec((tm_out, d), lambda t: (t, 0)),
            stack(mod),
            stack(ffn_g),
            _resident(w_out.shape, lambda t: (0, 0, 0)),
        ],
        out_specs=[pl.BlockSpec((tm_out, d), lambda t: (t, 0))] * 2,
        out_shape=[jax.ShapeDtypeStruct((s, d), F32), jax.ShapeDtypeStruct((s, d), BF16)],
        compiler_params=_params(("arbitrary",)),
        name="even_out_proj",
    )(cat, x2, mod, ffn_g, w_out)
    return x_new, h2, casts


def _pool_kernel(x_ref, mod_ref, g_ref, fg_ref, pw_ref, ps_ref, o_ref, h2_ref, hbuf_ref, p_ref, q_ref, *,
                 layer, group):
    tm, d = x_ref.shape
    n_groups = len(POOL_WINDOWS)
    dg = d // n_groups
    t0 = pl.program_id(0) * tm

    @pl.when(pl.program_id(0) == 0)
    def _():
        hbuf_ref[0:HALO, :] = jnp.zeros((HALO, d), F32)

    x = x_ref[...]
    m = mod_ref[...]
    shift, scale, gate = _mod_term(m, 0), _mod_term(m, 1), _mod_term(m, 2)
    h = _modulated_rmsnorm(x, _row(g_ref, layer), shift, scale)
    hbuf_ref[HALO:HALO + tm, :] = h

    end = HALO + tm
    p_ref[8:end, :] = hbuf_ref[8:end, :] + hbuf_ref[7:end - 1, :]
    q_ref[16:end, dg:] = p_ref[16:end, dg:] + p_ref[14:end - 2, dg:]
    p_ref[24:end, 2 * dg:] = q_ref[24:end, 2 * dg:] + q_ref[20:end - 4, 2 * dg:]
    q_ref[32:end, 3 * dg:] = p_ref[32:end, 3 * dg:] + p_ref[24:end - 8, 3 * dg:]
    sums = (p_ref, q_ref, p_ref, q_ref)

    pos = (lax.broadcasted_iota(jnp.int32, (tm, LANES), 0) + (t0 + 1)).astype(F32)
    ys = []
    for gi, w in enumerate(POOL_WINDOWS):
        lanes = slice(gi * dg, (gi + 1) * dg)
        inv_cnt = jnp.concatenate([1.0 / jnp.minimum(pos, float(w))] * (dg // LANES), axis=1)
        pooled = sums[gi][HALO:end, lanes] * inv_cnt - h[:, lanes]
        ys.append(_bdot(pooled.astype(BF16), pw_ref[gi * dg:(gi + 1) * dg, :]))
    y = jnp.concatenate(ys, axis=1) * _row(ps_ref, group)
    _store_residual_and_ffn_input(x + gate * y, m, _row(fg_ref, layer), o_ref, h2_ref)
    hbuf_ref[0:HALO, :] = hbuf_ref[tm:tm + HALO, :]


def _pool_layer(x2, mod, norm_g, ffn_g, pool_w, pool_scale, l, tm=512):
    s, d = x2.shape
    stack = lambda p: pl.BlockSpec(p.shape, lambda t: (0, 0))
    return pl.pallas_call(
        functools.partial(_pool_kernel, layer=l, group=l // 2),
        grid=(s // tm,),
        in_specs=[
            pl.BlockSpec((tm, d), lambda t: (t, 0)),
            stack(mod),
            stack(norm_g),
            stack(ffn_g),
            _resident(pool_w.shape, lambda t: (0, 0)),
            stack(pool_scale),
        ],
        out_specs=[pl.BlockSpec((tm, d), lambda t: (t, 0))] * 2,
        out_shape=[jax.ShapeDtypeStruct((s, d), F32), jax.ShapeDtypeStruct((s, d), BF16)],
        scratch_shapes=[pltpu.VMEM((HALO + tm, d), F32)] * 3,
        compiler_params=_params(("arbitrary",)),
        name="pool_mixer",
    )(x2, mod, norm_g, ffn_g, pool_w, pool_scale)


N_FFN_IN = 7


def _ffn_kernel(*refs, final_norm, n_cast, n_x, n_ada, ada_layer):
    h_ref, x_ref, mod_ref, w1_ref, w3_ref, w2_ref, fg_ref = refs[:N_FFN_IN]
    n_in = N_FFN_IN + (3 if n_ada else 0)
    cast_src = refs[n_in:n_in + n_cast]
    o_ref = refs[n_in + n_cast]
    n_out = 2 if n_ada else 1
    cast_dst = refs[n_in + n_cast + n_out:]
    j = pl.program_id(1)
    last = pl.num_programs(1) - 1
    x_rows = x_ref.shape[0]

    @pl.when(j == 0)
    def _():
        o_ref[...] = jnp.zeros(o_ref.shape, F32)

    if n_ada:
        ccol_ref, adaw_ref, adab_ref = refs[N_FFN_IN:n_in]
        modn_ref = refs[n_in + n_cast + 1]

        @pl.when(j == 0)
        def _():
            modn_ref[...] = _row(adab_ref, ada_layer)

        ccol = ccol_ref[...]
        part = jnp.sum(ccol * jax.nn.sigmoid(ccol) * adaw_ref[...], axis=0, keepdims=True)
        modn_ref[...] += jnp.where(j < n_ada, part, 0.0)

    for src, dst in zip(cast_src, cast_dst):
        dst[...] = src[...].astype(BF16)

    h = h_ref[...]
    tf = w1_ref.shape[1]
    us = []
    for lanes in (slice(0, tf // 2), slice(tf // 2, tf)):
        a = _bdot(h, w1_ref[:, lanes])
        b = _bdot(h, w3_ref[:, lanes])
        us.append((a * jax.nn.sigmoid(a) * b).astype(BF16))
    u = jnp.concatenate(us, axis=1)
    y = _mod_term(mod_ref[...], 5) * _bdot(u, w2_ref[...])
    for k in range(n_x):
        rows = slice(k * x_rows, (k + 1) * x_rows)
        o_ref[rows, :] += y[rows, :] + jnp.where(j == k, x_ref[...], 0.0)

    if final_norm:
        @pl.when(j == last)
        def _():
            out = o_ref[...]
            ms = jnp.mean(out * out, axis=-1, keepdims=True)
            o_ref[...] = out * lax.rsqrt(ms + EPS) * fg_ref[...]


def _cast_job(w, layer, n_i, n_j, column_blocks=False):
    rows, cols = w.shape[1:]
    if column_blocks:
        n_cb, rb = cols // W_BLOCK, rows // n_i
        assert n_cb <= n_j
        in_spec = pl.BlockSpec((None, rb, W_BLOCK), lambda t, j: (layer, t, jnp.minimum(j, n_cb - 1)))
        out_spec = pl.BlockSpec((None, rb, W_BLOCK), lambda t, j: (jnp.minimum(j, n_cb - 1), t, 0))
        return w, in_spec, out_spec, jax.ShapeDtypeStruct((n_cb, rows, W_BLOCK), BF16)
    if rows % (n_i * n_j * BF16_SUBLANES) == 0:
        rb = rows // (n_i * n_j)
        in_spec = pl.BlockSpec((None, rb, cols), lambda t, j: (layer, j * n_i + t, 0))
        out_spec = pl.BlockSpec((rb, cols), lambda t, j: (j * n_i + t, 0))
    else:
        n_cb = max(k for k in range(1, n_j + 1) if cols % (k * LANES) == 0)
        rb, cb = rows // n_i, cols // n_cb
        in_spec = pl.BlockSpec((None, rb, cb), lambda t, j: (layer, t, jnp.minimum(j, n_cb - 1)))
        out_spec = pl.BlockSpec((rb, cb), lambda t, j: (t, jnp.minimum(j, n_cb - 1)))
    return w, in_spec, out_spec, jax.ShapeDtypeStruct((rows, cols), BF16)


def _ffn_layer(h2, x2, mod, w1, w3, w2, final_g, final_norm, next_ada=None, cast_ahead=(), tm=1024, tf=512):
    s, d = x2.shape
    d_ff = w1.shape[1]
    n_i, n_j = s // tm, d_ff // tf
    n_x = max(k for k in range(1, n_j + 1) if tm % (k * SUBLANES) == 0)
    jobs = [_cast_job(w, layer, n_i, n_j, blocks) for w, layer, blocks in cast_ahead]
    ada_in, ada_specs, ada_out_specs, ada_out_shapes, n_ada, ada_layer = [], [], [], [], 0, None
    if next_ada is not None:
        c_cols, ada_w, ada_b, ada_layer = next_ada
        n_ada, kb, _ = c_cols.shape
        n_mod = ada_w.shape[2]
        cb = n_mod // n_i
        assert n_ada <= n_j and cb % LANES == 0
        kblk = lambda j: jnp.minimum(j, n_ada - 1)
        ada_in = [c_cols, ada_w, ada_b]
        ada_specs = [
            pl.BlockSpec((None, kb, 1), lambda t, j: (kblk(j), 0, 0)),
            pl.BlockSpec((None, kb, cb), lambda t, j: (ada_layer, kblk(j), t)),
            pl.BlockSpec((ada_b.shape[0], cb), lambda t, j: (0, t)),
        ]
        ada_out_specs = [pl.BlockSpec((1, cb), lambda t, j: (0, t))]
        ada_out_shapes = [jax.ShapeDtypeStruct((1, n_mod), F32)]
    outs = pl.pallas_call(
        functools.partial(_ffn_kernel, final_norm=final_norm, n_cast=len(jobs), n_x=n_x, n_ada=n_ada,
                          ada_layer=ada_layer),
        grid=(n_i, n_j),
        in_specs=[
            pl.BlockSpec((tm, d), lambda t, j: (t, 0)),
            pl.BlockSpec((tm // n_x, d), lambda t, j: (t * n_x + jnp.minimum(j, n_x - 1), 0)),
            pl.BlockSpec(mod.shape, lambda t, j: (0, 0)),
            pl.BlockSpec((d, tf), lambda t, j: (0, j)),
            pl.BlockSpec((d, tf), lambda t, j: (0, j)),
            pl.BlockSpec((tf, d), lambda t, j: (j, 0)),
            pl.BlockSpec((1, d), lambda t, j: (0, 0)),
        ] + ada_specs + [job[1] for job in jobs],
        out_specs=[pl.BlockSpec((tm, d), lambda t, j: (t, 0))] + ada_out_specs + [job[2] for job in jobs],
        out_shape=[jax.ShapeDtypeStruct((s, d), F32)] + ada_out_shapes + [job[3] for job in jobs],
        compiler_params=_params(("arbitrary", "arbitrary")),
        name="ffn",
    )(h2, x2, mod, w1, w3, w2, final_g, *ada_in, *[job[0] for job in jobs])
    n_fixed = 1 + len(ada_out_shapes)
    mod_next = outs[1] if next_ada is not None else None
    return outs[0], mod_next, outs[n_fixed:]


def kernel(x, c, ada_w, ada_b, norm_mix_g, norm_ffn_g, ab_w_in, a_conv_w, a_conv_b, a_norm_g, a_norm_b,
           b_norm_g, b_norm_b, b_w_s, b_bias, ab_w_out, pool_w, pool_scale, ffn_w1, ffn_w3, ffn_w2, final_g):
    bsz, s, d = x.shape
    depth = ada_w.shape[0]
    assert bsz == 1, "per-sequence adaLN terms are passed as single rows"

    n_odd, n_groups, dg, _ = pool_w.shape
    pool_w2 = pool_w.reshape(n_odd, n_groups * dg, dg)
    mod, (w_in, w_out, *p_ws) = _adaln_first(
        c, ada_w, ada_b,
        [(ab_w_in, 0, True), (ab_w_out, 0, True)] + [(pool_w2, i, False) for i in range(n_odd)])
    c_cols = c.reshape(ADA_ROW_BLOCKS, d // ADA_ROW_BLOCKS, 1)
    n_even, n_heads, chunk = b_bias.shape
    bias_full = jnp.broadcast_to(
        jnp.transpose(b_bias, (0, 2, 1))[:, :, :, None], (n_even, chunk, n_heads, HEAD_DIM_B)
    ).reshape(n_even, chunk, n_heads * HEAD_DIM_B)

    w1 = w3 = w2 = None
    fg = final_g.reshape(1, d)

    x2 = x.reshape(s, d)
    for l in range(depth):
        if l % 2 == 0:
            first = l == 0
            x2, h2, casts_m = _even_layer(
                x2, mod, norm_mix_g, norm_ffn_g, w_in, a_conv_w, a_conv_b, a_norm_g, a_norm_b, b_norm_g, b_norm_b,
                b_w_s, bias_full, w_out, l,
                mixer_casts=[(ffn_w1, 0, False), (ffn_w3, 0, False), (ffn_w2, 0, False)] if first else ())
            if first:
                w1, w3, w2 = casts_m
        else:
            x2, h2 = _pool_layer(x2, mod, norm_mix_g, norm_ffn_g, p_ws[l // 2], pool_scale, l)
        cast_ahead, next_ada = [], None
        if l + 1 < depth:
            next_ada = (c_cols, ada_w, ada_b, l + 1)
            cast_ahead += [(ffn_w1, l + 1, False), (ffn_w3, l + 1, False), (ffn_w2, l + 1, False)]
            if (l + 1) % 2 == 0:
                cast_ahead += [(ab_w_in, (l + 1) // 2, True), (ab_w_out, (l + 1) // 2, True)]
        x2, mod, casts = _ffn_layer(h2, x2, mod, w1, w3, w2, fg, final_norm=(l == depth - 1),
                                    next_ada=next_ada, cast_ahead=cast_ahead)
        if casts:
            w1, w3, w2 = casts[:3]
            if len(casts) > 3:
                w_in, w_out = casts[3:]
    return x2.reshape(bsz, s, d)
```
